```python
import math
import jax, jax.numpy as jnp
from jax import lax
import numpy as np

D_MODEL = 1024
BATCH = 4
SEQ = 4096
DEPTH = 2

N_A_LAYERS = DEPTH // 2
N_B_LAYERS = DEPTH - N_A_LAYERS
S5_GROUP = 16
S5_GROUPS = D_MODEL // S5_GROUP
S5_STATE = 64
DT_MIN = 0.001
DT_MAX = 0.1
SB_HEADS = 16
SB_HEAD_DIM = D_MODEL // SB_HEADS
Q_BLOCK = 128
D_FF = 4 * D_MODEL
DEEPNORM_ALPHA = (2.0 * DEPTH) ** 0.25
DEEPNORM_BETA = (8.0 * DEPTH) ** -0.25
LN_EPS = 1e-5

kernel_name = "s5_stickbreaking_yoco_deepnorm"


def layer_norm(x, g, b):
    xf = x.astype(jnp.float32)
    mu = jnp.mean(xf, axis=-1, keepdims=True)
    var = jnp.mean(jnp.square(xf - mu), axis=-1, keepdims=True)
    y = (xf - mu) * lax.rsqrt(var + LN_EPS)
    return (y * g.astype(jnp.float32) + b.astype(jnp.float32)).astype(x.dtype)


def s5_mixer(x, w_in, lam_re, lam_im, b_re, b_im, c_re, c_im, d_skip, log_step, w_glu, b_glu, w_out):
    bsz, seq, _ = x.shape
    u = (x @ w_in).astype(jnp.float32).reshape(bsz, seq, S5_GROUPS, S5_GROUP)
    dt = jnp.exp(log_step.astype(jnp.float32))[:, None]
    lr = lam_re.astype(jnp.float32)
    li = lam_im.astype(jnp.float32)
    mag = jnp.exp(lr * dt)
    ang = li * dt
    ab_re = mag * jnp.cos(ang)
    ab_im = mag * jnp.sin(ang)
    nr = ab_re - 1.0
    ni = ab_im
    den = lr * lr + li * li
    f_re = (nr * lr + ni * li) / den
    f_im = (ni * lr - nr * li) / den
    br = b_re.astype(jnp.float32)
    bi = b_im.astype(jnp.float32)
    bb_re = f_re[..., None] * br - f_im[..., None] * bi
    bb_im = f_re[..., None] * bi + f_im[..., None] * br
    bu_re = jnp.einsum('blgh,gph->lbgp', u, bb_re)
    bu_im = jnp.einsum('blgh,gph->lbgp', u, bb_im)
    a_re = jnp.broadcast_to(ab_re[None, None], (seq, 1, S5_GROUPS, S5_STATE))
    a_im = jnp.broadcast_to(ab_im[None, None], (seq, 1, S5_GROUPS, S5_STATE))

    def combine(e_i, e_j):
        ar_i, ai_i, sr_i, si_i = e_i
        ar_j, ai_j, sr_j, si_j = e_j
        return (ar_j * ar_i - ai_j * ai_i,
                ar_j * ai_i + ai_j * ar_i,
                ar_j * sr_i - ai_j * si_i + sr_j,
                ar_j * si_i + ai_j * sr_i + si_j)

    _, _, s_re, s_im = lax.associative_scan(combine, (a_re, a_im, bu_re, bu_im), axis=0)
    y = (jnp.einsum('lbgp,ghp->blgh', s_re, c_re.astype(jnp.float32))
         - jnp.einsum('lbgp,ghp->blgh', s_im, c_im.astype(jnp.float32)))
    y = (y + d_skip.astype(jnp.float32).reshape(S5_GROUPS, S5_GROUP) * u).reshape(bsz, seq, D_MODEL)
    g = jax.nn.gelu(y)
    out = g * jax.nn.sigmoid(g @ w_glu.astype(jnp.float32) + b_glu.astype(jnp.float32))
    return out.astype(x.dtype) @ w_out


def shared_kv(x, w_kv):
    bsz, seq, _ = x.shape
    kv = x @ w_kv
    k, v = jnp.split(kv, 2, axis=-1)
    k = k.reshape(bsz, seq, SB_HEADS, SB_HEAD_DIM).transpose(0, 2, 1, 3)
    v = v.reshape(bsz, seq, SB_HEADS, SB_HEAD_DIM).transpose(0, 2, 1, 3)
    return k, v


def stick_breaking_attention(x, k, v, w_q, w_out):
    bsz, seq, _ = x.shape
    q = (x @ w_q).reshape(bsz, seq, SB_HEADS, SB_HEAD_DIM).transpose(0, 2, 1, 3)
    scale = 1.0 / math.sqrt(SB_HEAD_DIM)
    outs = []
    for blk in range(seq // Q_BLOCK):
        start = blk * Q_BLOCK
        end = start + Q_BLOCK
        qb = q[:, :, start:end]
        kb = k[:, :, :end]
        vb = v[:, :, :end]
        z = jnp.einsum('bhqd,bhkd->bhqk', qb, kb, preferred_element_type=jnp.float32) * scale
        t_pos = start + jnp.arange(Q_BLOCK)
        s_pos = jnp.arange(end)
        causal = s_pos[None, :] < t_pos[:, None]
        log_1mb = jnp.where(causal, jax.nn.log_sigmoid(-z), 0.0)
        suffix = lax.cumsum(log_1mb, axis=3, reverse=True) - log_1mb
        w = jnp.where(causal, jnp.exp(jax.nn.log_sigmoid(z) + suffix), 0.0)
        outs.append(jnp.einsum('bhqk,bhkd->bhqd', w.astype(vb.dtype), vb))
    o = jnp.concatenate(outs, axis=2).transpose(0, 2, 1, 3).reshape(bsz, seq, D_MODEL)
    return o @ w_out


def squared_relu_mlp(x, w1, b1, w2, b2):
    h = jnp.square(jax.nn.relu(x @ w1 + b1))
    return h @ w2 + b2


def setup_inputs(seed: int = 0) -> dict:
    key = jax.random.key(seed)
    ks = jax.random.split(key, 24)
    D, G, P, H = D_MODEL, S5_GROUPS, S5_STATE, S5_GROUP
    nA, nB = N_A_LAYERS, N_B_LAYERS
    nrm = jax.random.normal
    x = nrm(ks[0], (BATCH, SEQ, D), jnp.float32)
    s5_w_in = nrm(ks[1], (nA, D, D), jnp.float32) * D ** -0.5
    s5_lambda_re = -0.5 + 0.01 * nrm(ks[2], (nA, G, P), jnp.float32)
    s5_lambda_im = (jnp.pi * jnp.arange(P, dtype=jnp.float32))[None, None, :] + 0.01 * nrm(ks[3], (nA, G, P), jnp.float32)
    s5_b_re = nrm(ks[4], (nA, G, P, H), jnp.float32) * (2.0 * H) ** -0.5
    s5_b_im = nrm(ks[5], (nA, G, P, H), jnp.float32) * (2.0 * H) ** -0.5
    s5_c_re = nrm(ks[6], (nA, G, H, P), jnp.float32) * (2.0 * P) ** -0.5
    s5_c_im = nrm(ks[7], (nA, G, H, P), jnp.float32) * (2.0 * P) ** -0.5
    s5_d = nrm(ks[8], (nA, D), jnp.float32)
    s5_log_step = jax.random.uniform(ks[9], (nA, G), jnp.float32, math.log(DT_MIN), math.log(DT_MAX))
    s5_w_glu = nrm(ks[10], (nA, D, D), jnp.float32) * D ** -0.5
    s5_b_glu = 0.02 * nrm(ks[11], (nA, D), jnp.float32)
    s5_w_out = nrm(ks[12], (nA, D, D), jnp.float32) * D ** -0.5 * DEEPNORM_BETA
    w_k = nrm(ks[13], (D, D), jnp.float32) * D ** -0.5
    w_v = nrm(ks[14], (D, D), jnp.float32) * D ** -0.5 * DEEPNORM_BETA
    sb_w_kv = jnp.concatenate([w_k, w_v], axis=1)
    sb_w_q = nrm(ks[15], (nB, D, D), jnp.float32) * D ** -0.5
    sb_w_out = nrm(ks[16], (nB, D, D), jnp.float32) * D ** -0.5 * DEEPNORM_BETA
    mlp_w1 = nrm(ks[17], (DEPTH, D, D_FF), jnp.float32) * D ** -0.5
    mlp_b1 = 0.02 * nrm(ks[18], (DEPTH, D_FF), jnp.float32)
    mlp_w2 = nrm(ks[19], (DEPTH, D_FF, D), jnp.float32) * D_FF ** -0.5 * DEEPNORM_BETA
    mlp_b2 = 0.02 * nrm(ks[20], (DEPTH, D), jnp.float32)
    lk = jax.random.split(ks[21], 4)
    ln_mix_g = 1.0 + 0.05 * nrm(lk[0], (DEPTH, D), jnp.float32)
    ln_mix_b = 0.02 * nrm(lk[1], (DEPTH, D), jnp.float32)
    ln_mlp_g = 1.0 + 0.05 * nrm(lk[2], (DEPTH, D), jnp.float32)
    ln_mlp_b = 0.02 * nrm(lk[3], (DEPTH, D), jnp.float32)
    return {"x": x,
            "s5_w_in": s5_w_in, "s5_lambda_re": s5_lambda_re, "s5_lambda_im": s5_lambda_im,
            "s5_b_re": s5_b_re, "s5_b_im": s5_b_im, "s5_c_re": s5_c_re, "s5_c_im": s5_c_im,
            "s5_d": s5_d, "s5_log_step": s5_log_step, "s5_w_glu": s5_w_glu, "s5_b_glu": s5_b_glu,
            "s5_w_out": s5_w_out,
            "sb_w_kv": sb_w_kv, "sb_w_q": sb_w_q, "sb_w_out": sb_w_out,
            "mlp_w1": mlp_w1, "mlp_b1": mlp_b1, "mlp_w2": mlp_w2, "mlp_b2": mlp_b2,
            "ln_mix_g": ln_mix_g, "ln_mix_b": ln_mix_b, "ln_mlp_g": ln_mlp_g, "ln_mlp_b": ln_mlp_b}


def reference(x, s5_w_in, s5_lambda_re, s5_lambda_im, s5_b_re, s5_b_im, s5_c_re, s5_c_im,
              s5_d, s5_log_step, s5_w_glu, s5_b_glu, s5_w_out,
              sb_w_kv, sb_w_q, sb_w_out,
              mlp_w1, mlp_b1, mlp_w2, mlp_b2,
              ln_mix_g, ln_mix_b, ln_mlp_g, ln_mlp_b):
    k_shared = None
    v_shared = None
    for layer in range(DEPTH):
        if layer < N_A_LAYERS:
            a = layer
            mix = s5_mixer(x, s5_w_in[a], s5_lambda_re[a], s5_lambda_im[a], s5_b_re[a], s5_b_im[a],
                           s5_c_re[a], s5_c_im[a], s5_d[a], s5_log_step[a], s5_w_glu[a], s5_b_glu[a],
                           s5_w_out[a])
        else:
            if layer == N_A_LAYERS:
                k_shared, v_shared = shared_kv(x, sb_w_kv)
            bi = layer - N_A_LAYERS
            mix = stick_breaking_attention(x, k_shared, v_shared, sb_w_q[bi], sb_w_out[bi])
        x = layer_norm(DEEPNORM_ALPHA * x + mix, ln_mix_g[layer], ln_mix_b[layer])
        ff = squared_relu_mlp(x, mlp_w1[layer], mlp_b1[layer], mlp_w2[layer], mlp_b2[layer])
        x = layer_norm(DEEPNORM_ALPHA * x + ff, ln_mlp_g[layer], ln_mlp_b[layer])
    return x
```

```python
import functools
import math

import jax
import jax.numpy as jnp
from jax import lax
from jax.experimental import pallas as pl
from jax.experimental.pallas import tpu as pltpu

F32 = jnp.float32
BF16 = jnp.bfloat16

D_MODEL = 1024
DEPTH = 2
S5_GROUP = 16
S5_GROUPS = D_MODEL // S5_GROUP
S5_STATE = 64
SB_HEADS = 16
SB_HEAD_DIM = D_MODEL // SB_HEADS
D_FF = 4 * D_MODEL
DEEPNORM_ALPHA = (2.0 * DEPTH) ** 0.25
LN_EPS = 1e-5

LANES = 128
SUBLANES = 8
KB = 16
GROUPS_PER_TILE = LANES // S5_GROUP
N_TILES = D_MODEL // LANES
TILE_STATE = GROUPS_PER_TILE * S5_STATE
VMEM_LIMIT = 56 * 1024 * 1024
ATT_BLOCK = 128
EXP_FLOOR = -90.0


def _params(*sem):
    return pltpu.CompilerParams(dimension_semantics=sem, vmem_limit_bytes=VMEM_LIMIT)


def _resident(shape):
    zeros = (0,) * len(shape)
    return pl.BlockSpec(shape, lambda *_: zeros, pipeline_mode=pl.Buffered(1))


def _cmul(ar, ai, br, bi):
    return ar * br - ai * bi, ar * bi + ai * br


def _s5_prep_kernel(lr_ref, li_ref, ls_ref, br_ref, bi_ref, cr_ref, ci_ref,
                    car_ref, cai_ref, wr_ref, wi_ref, pwr_ref, pwi_ref,
                    a1r_ref, a1i_ref, pr_ref, pi_ref, bbr_ref, bbi_ref):
    tau = pl.program_id(0)

    @pl.when(tau == 0)
    def _():
        lr = lr_ref[...]
        li = li_ref[...]
        dt = jnp.exp(ls_ref[...])
        mag = jnp.exp(lr * dt)
        ang = li * dt
        a_re = mag * jnp.cos(ang)
        a_im = mag * jnp.sin(ang)
        nr = a_re - 1.0
        ni = a_im
        den = lr * lr + li * li
        f_re = (nr * lr + ni * li) / den
        f_im = (ni * lr - nr * li) / den
        bbr, bbi = _cmul(f_re[None], f_im[None], br_ref[...], bi_ref[...])
        bbr_ref[...] = bbr
        bbi_ref[...] = bbi
        a1r_ref[...] = a_re
        a1i_ref[...] = a_im
        pr_ref[...] = jnp.ones_like(a_re)
        pi_ref[...] = jnp.zeros_like(a_im)

    p_re = pr_ref[...]
    p_im = pi_ref[...]
    car, cai = _cmul(cr_ref[...], ci_ref[...], p_re[None], p_im[None])
    car_ref[...] = car
    cai_ref[...] = -cai
    wr, wi = _cmul(p_re[None], p_im[None], bbr_ref[...], bbi_ref[...])
    wr_ref[...] = wr
    wi_ref[...] = wi

    @pl.when(tau == KB)
    def _():
        q_re, q_im = p_re, p_im
        for i in range(SUBLANES):
            pwr_ref[i] = q_re
            pwi_ref[i] = q_im
            q_re, q_im = _cmul(q_re, q_im, p_re, p_im)

    n_re, n_im = _cmul(p_re, p_im, a1r_ref[...], a1i_ref[...])
    pr_ref[...] = n_re
    pi_ref[...] = n_im


def _s5_prep(lam_re, lam_im, log_step, b_re, b_im, c_re, c_im):
    g, p, h = S5_GROUPS, S5_STATE, S5_GROUP
    rows = g * p // LANES
    dense = lambda a: a.reshape(a.shape[:-2] + (rows, LANES))
    lr = dense(lam_re)
    li = dense(lam_im)
    ls = dense(jnp.broadcast_to(log_step[:, None], (g, p)))
    br = dense(b_re.transpose(2, 0, 1))
    bi = dense(b_im.transpose(2, 0, 1))
    cr = dense(c_re.transpose(1, 0, 2))
    ci = dense(c_im.transpose(1, 0, 2))
    small = pl.BlockSpec((rows, LANES), lambda t: (0, 0))
    big = pl.BlockSpec((h, rows, LANES), lambda t: (0, 0, 0))
    step = pl.BlockSpec((None, h, rows, LANES), lambda t: (t, 0, 0, 0))
    pw = pl.BlockSpec((SUBLANES, rows, LANES), lambda t: (0, 0, 0))
    out4 = jax.ShapeDtypeStruct((KB + 1, h, rows, LANES), F32)
    outp = jax.ShapeDtypeStruct((SUBLANES, rows, LANES), F32)
    car, cai, wr, wi, pwr, pwi = pl.pallas_call(
        _s5_prep_kernel,
        grid=(KB + 1,),
        in_specs=[small, small, small, big, big, big, big],
        out_specs=[step, step, step, step, pw, pw],
        out_shape=[out4, out4, out4, out4, outp, outp],
        scratch_shapes=[pltpu.VMEM((rows, LANES), F32)] * 4 + [pltpu.VMEM((h, rows, LANES), F32)] * 2,
        compiler_params=_params("arbitrary"),
        name="s5_prep",
    )(lr, li, ls, br, bi, cr, ci)
    undense = lambda a: a.reshape(a.shape[:-2] + (g, p))
    return undense(car), undense(cai), undense(wr), undense(wi), undense(pwr), undense(pwi)


def _s5_lag_kernel(car_ref, cai_ref, bbr_ref, bbi_ref, k_ref):
    nt = (((1,), (1,)), ((), ()))
    for gi in range(car_ref.shape[0]):
        k_ref[gi] = (lax.dot_general(car_ref[gi], bbr_ref[gi], nt, precision=lax.Precision.HIGHEST,
                                     preferred_element_type=F32)
                     + lax.dot_general(cai_ref[gi], bbi_ref[gi], nt, precision=lax.Precision.HIGHEST,
                                       preferred_element_type=F32))


def _s5_lag_kernels(car, cai, bbr, bbi):
    g, p, h = S5_GROUPS, S5_STATE, S5_GROUP
    gs = GROUPS_PER_TILE
    return pl.pallas_call(
        _s5_lag_kernel,
        grid=(g // gs,),
        in_specs=[pl.BlockSpec((gs, KB * h, p), lambda i: (i, 0, 0)),
                  pl.BlockSpec((gs, KB * h, p), lambda i: (i, 0, 0)),
                  pl.BlockSpec((gs, h, p), lambda i: (i, 0, 0)),
                  pl.BlockSpec((gs, h, p), lambda i: (i, 0, 0))],
        out_specs=pl.BlockSpec((gs, KB * h, h), lambda i: (i, 0, 0)),
        out_shape=jax.ShapeDtypeStruct((g, KB * h, h), F32),
        compiler_params=_params("arbitrary"),
        name="s5_lag_kernels",
    )(car, cai, bbr, bbi)


def _s5_operators(lam_re, lam_im, log_step, b_re, b_im, c_re, c_im):
    g, p, h = S5_GROUPS, S5_STATE, S5_GROUP
    nt, gt = N_TILES, GROUPS_PER_TILE
    car, cai, wr, wi, pwr, pwi = _s5_prep(lam_re, lam_im, log_step, b_re, b_im, c_re, c_im)
    lag_l = lambda a: a[:KB].transpose(2, 0, 1, 3).reshape(g, KB * h, p)
    lag_r = lambda a: a[0].transpose(1, 0, 2)
    k = _s5_lag_kernels(lag_l(car), lag_l(cai), lag_r(wr), lag_r(wi))
    k = k.reshape(g, KB, h, h)
    same_group = jnp.eye(gt, dtype=bool)
    j = jnp.arange(KB)
    lag = j[None, :] - j[:, None]
    kt = jnp.where((lag >= 0)[None, :, :, None, None], k[:, jnp.clip(lag, 0, KB - 1)], 0.0)
    kt = kt.reshape(nt, gt, KB, KB, h, h).transpose(0, 2, 1, 5, 3, 4)
    t_op = jnp.where(same_group[None, None, :, None, None, :, None], kt[:, :, :, :, :, None, :], 0.0)
    t_op = t_op.reshape(nt, KB * LANES, KB * LANES).astype(BF16)
    w = jnp.stack([wr[:KB][::-1], wi[:KB][::-1]])
    w = w.reshape(2, KB, h, nt, gt, p).transpose(3, 1, 4, 2, 0, 5)
    w_in = jnp.where(same_group[None, None, :, None, None, :, None], w[:, :, :, :, :, None, :], 0.0)
    w_in = w_in.reshape(nt, KB * LANES, 2 * TILE_STATE).astype(BF16)
    v = jnp.stack([car[1:], cai[1:]])
    v = v.reshape(2, KB, h, nt, gt, p).transpose(3, 0, 4, 5, 1, 2)
    v_out = jnp.where(same_group[None, None, :, None, None, :, None], v[:, :, :, :, :, None, :], 0.0)
    v_out = v_out.reshape(nt, 2 * TILE_STATE, KB * LANES).astype(BF16)
    pw = lambda a: a.reshape(SUBLANES, nt, TILE_STATE).transpose(1, 0, 2)
    return t_op, w_in, v_out, pw(pwr), pw(pwi)


def _uproj_kernel(x_ref, w_ref, o_ref):
    o_ref[...] = jnp.dot(x_ref[...].astype(BF16), w_ref[...], preferred_element_type=F32)


def _u_proj(xv, w_in):
    rows = xv.shape[0]
    tm = min(rows, 512)
    return pl.pallas_call(
        _uproj_kernel,
        grid=(rows // tm, KB),
        in_specs=[pl.BlockSpec((tm, D_MODEL), lambda i, j: (i, j)), _resident((D_MODEL, D_MODEL))],
        out_specs=pl.BlockSpec((None, tm, D_MODEL), lambda i, j: (j, i, 0)),
        out_shape=jax.ShapeDtypeStruct((KB, rows, D_MODEL), F32),
        compiler_params=_params("parallel", "arbitrary"),
        name="s5_u_proj",
    )(xv, w_in)


def _shift_rows(x, k, row):
    return jnp.where(row >= k, pltpu.roll(x, k, 0), 0.0)


def _ssm_kernel(u_ref, t_ref, win_ref, vout_ref, pwr_ref, pwi_ref, d_ref, y_ref, s_ref):
    nb = u_ref.shape[1]
    ts = TILE_STATE
    u_cat = jnp.concatenate([u_ref[j] for j in range(KB)], axis=1)
    u_bf = u_cat.astype(BF16)
    s_ref[...] = jnp.dot(u_bf, win_ref[...], preferred_element_type=F32)
    row = lax.broadcasted_iota(jnp.int32, (SUBLANES, ts), 0)
    pw_re = pwr_ref[...]
    pw_im = pwi_ref[...]

    def group(r, carry):
        c_re, c_im = carry
        rows = pl.ds(pl.multiple_of(r * SUBLANES, SUBLANES), SUBLANES)
        x_re = s_ref[rows, 0:ts]
        x_im = s_ref[rows, ts:2 * ts]
        for k in (1, 2, 4):
            a_re = pw_re[k - 1:k]
            a_im = pw_im[k - 1:k]
            sh_re = _shift_rows(x_re, k, row)
            sh_im = _shift_rows(x_im, k, row)
            x_re, x_im = x_re + a_re * sh_re - a_im * sh_im, x_im + a_re * sh_im + a_im * sh_re
        x_re, x_im = x_re + pw_re * c_re - pw_im * c_im, x_im + pw_re * c_im + pw_im * c_re
        s_ref[rows, 0:ts] = jnp.where(row >= 1, pltpu.roll(x_re, 1, 0), c_re)
        s_ref[rows, ts:2 * ts] = jnp.where(row >= 1, pltpu.roll(x_im, 1, 0), c_im)
        return x_re[SUBLANES - 1:SUBLANES], x_im[SUBLANES - 1:SUBLANES]

    zero = jnp.zeros((1, ts), F32)
    lax.fori_loop(0, nb // SUBLANES, group, (zero, zero))
    s_bf = s_ref[...].astype(BF16)
    width = 2 * LANES
    for c in range(KB * LANES // width):
        cols = slice(c * width, (c + 1) * width)
        live = (c + 1) * width
        y = jnp.dot(u_bf[:, :live], t_ref[0:live, cols], preferred_element_type=F32)
        y = y + jnp.dot(s_bf, vout_ref[:, cols], preferred_element_type=F32)
        y = y + d_ref[:, cols] * u_cat[:, cols]
        for jj in range(width // LANES):
            y_ref[c * (width // LANES) + jj] = y[:, jj * LANES:(jj + 1) * LANES]


def _ssm(u3, t_op, w_in, v_out, pwr, pwi, d_tiled, batch):
    rows = u3.shape[1]
    nb = rows // batch
    tile = lambda shape: pl.BlockSpec((None,) + shape, lambda c, b: (c, 0, 0))
    io = pl.BlockSpec((KB, nb, LANES), lambda c, b: (0, b, c))
    return pl.pallas_call(
        _ssm_kernel,
        grid=(N_TILES, batch),
        in_specs=[io, tile((KB * LANES, KB * LANES)), tile((KB * LANES, 2 * TILE_STATE)),
                  tile((2 * TILE_STATE, KB * LANES)), tile((SUBLANES, TILE_STATE)),
                  tile((SUBLANES, TILE_STATE)), tile((1, KB * LANES))],
        out_specs=io,
        out_shape=jax.ShapeDtypeStruct(u3.shape, F32),
        scratch_shapes=[pltpu.VMEM((nb, 2 * TILE_STATE), F32)],
        compiler_params=_params("parallel", "arbitrary"),
        name="s5_ssm",
    )(u3, t_op, w_in, v_out, pwr, pwi, d_tiled)


def _layer_norm(z, g, b):
    mu = jnp.mean(z, axis=-1, keepdims=True)
    zc = z - mu
    var = jnp.mean(zc * zc, axis=-1, keepdims=True)
    return zc * lax.rsqrt(var + LN_EPS) * g + b


def _tail_kernel(*refs, glu):
    if glu:
        (m_ref, x_ref, wglu_ref, bglu_ref, wout_ref, g1_ref, b1n_ref,
         w1_ref, b1_ref, w2_ref, b2_ref, g2_ref, b2n_ref, o_ref) = refs
        g = jax.nn.gelu(m_ref[...])
        gate = jnp.dot(g.astype(BF16), wglu_ref[...], preferred_element_type=F32) + bglu_ref[...]
        m = (g * jax.nn.sigmoid(gate)).astype(BF16)
    else:
        (m_ref, x_ref, wout_ref, g1_ref, b1n_ref,
         w1_ref, b1_ref, w2_ref, b2_ref, g2_ref, b2n_ref, o_ref) = refs
        m = m_ref[...]
    mix = jnp.dot(m, wout_ref[...], preferred_element_type=F32)
    x1 = _layer_norm(DEEPNORM_ALPHA * x_ref[...] + mix, g1_ref[...], b1n_ref[...])
    x1_bf = x1.astype(BF16)
    ff = jnp.zeros_like(x1)
    for c in range(D_FF // D_MODEL):
        cols = slice(c * D_MODEL, (c + 1) * D_MODEL)
        hid = jnp.dot(x1_bf, w1_ref[:, cols], preferred_element_type=F32) + b1_ref[:, cols]
        hid = jnp.square(jnp.maximum(hid, 0.0))
        ff = ff + jnp.dot(hid.astype(BF16), w2_ref[cols, :], preferred_element_type=F32)
    ff = ff + b2_ref[...]
    o_ref[...] = _layer_norm(DEEPNORM_ALPHA * x1 + ff, g2_ref[...], b2n_ref[...])


def _layer_tail(m, x, m_spec, x_spec, out_spec, out_shape, grid, weights, glu):
    vec = lambda a: a.reshape(1, -1)
    ops = [m, x]
    specs = [m_spec, x_spec]
    for wgt in weights:
        a = vec(wgt) if wgt.ndim == 1 else wgt
        ops.append(a)
        specs.append(_resident(a.shape))
    return pl.pallas_call(
        functools.partial(_tail_kernel, glu=glu),
        grid=grid,
        in_specs=specs,
        out_specs=out_spec,
        out_shape=out_shape,
        compiler_params=_params(*(("parallel",) * len(grid))),
        name="layer_tail_glu" if glu else "layer_tail",
    )(*ops)


def _qkv_kernel(x_ref, w_ref, q_ref, k_ref, v_ref):
    y = jnp.dot(x_ref[...].astype(BF16), w_ref[...], preferred_element_type=F32)
    q_ref[...] = (y[:, :D_MODEL] * (1.0 / math.sqrt(SB_HEAD_DIM))).astype(BF16)
    k_ref[...] = y[:, D_MODEL:2 * D_MODEL].astype(BF16)
    v_ref[...] = y[:, 2 * D_MODEL:].astype(BF16)


def _qkv(x, w_qkv):
    n = x.shape[0]
    tm = min(n, 512)
    row = pl.BlockSpec((tm, D_MODEL), lambda i: (i, 0))
    out = jax.ShapeDtypeStruct((n, D_MODEL), BF16)
    return pl.pallas_call(
        _qkv_kernel,
        grid=(n // tm,),
        in_specs=[row, _resident(w_qkv.shape)],
        out_specs=[row, row, row],
        out_shape=[out, out, out],
        compiler_params=_params("parallel"),
        name="qkv_proj",
    )(x, w_qkv)


def _attn_kernel(q_ref, k_ref, v_ref, tri_ref, ones_ref, o_ref, acc_ref, r_ref, kn_ref):
    i = pl.program_id(2)
    tb = ATT_BLOCK
    lane = lax.broadcasted_iota(jnp.int32, (tb, LANES), 1)
    first = lane < SB_HEAD_DIM
    nt = (((1,), (1,)), ((), ()))

    def split_heads(a):
        zero = jnp.zeros_like(a)
        return jnp.concatenate([jnp.where(first, a, zero), jnp.where(first, zero, a)], axis=0)

    def head_sq_norm_max(a):
        sq = jnp.square(a.astype(F32))
        n0 = jnp.sum(jnp.where(first[:a.shape[0]], sq, 0.0), axis=1, keepdims=True)
        n1 = jnp.sum(jnp.where(first[:a.shape[0]], 0.0, sq), axis=1, keepdims=True)
        return jnp.max(jnp.maximum(n0, n1))

    @pl.when(i == 0)
    def _():
        def part(c, m):
            blk = k_ref[pl.ds(pl.multiple_of(c * tb, tb), tb), :]
            return jnp.maximum(m, head_sq_norm_max(blk))
        kn_ref[0] = lax.fori_loop(0, k_ref.shape[0] // tb, part, jnp.float32(0.0))

    q = q_ref[...]
    z_bound = jnp.sqrt(head_sq_norm_max(q) * kn_ref[0]) * 1.01 + 1e-3
    acc_ref[...] = jnp.zeros_like(acc_ref)
    r_ref[...] = jnp.zeros_like(r_ref)

    def block(j, diagonal):
        keys = pl.ds(pl.multiple_of(j * tb, tb), tb)
        z = lax.dot_general(q, split_heads(k_ref[keys, :]), nt, preferred_element_type=F32)
        log_1mb = -(jnp.maximum(z, 0.0) + jnp.log(1.0 + jnp.exp(-jnp.abs(z))))
        if diagonal:
            rr = lax.broadcasted_iota(jnp.int32, (tb, 2 * tb), 0)
            cc = lax.broadcasted_iota(jnp.int32, (tb, 2 * tb), 1)
            causal = jnp.bitwise_and(cc, tb - 1) < rr
            log_1mb = jnp.where(causal, log_1mb, 0.0)
        hi = log_1mb.astype(BF16)
        lo = (log_1mb - hi.astype(F32)).astype(BF16)
        suffix = (jnp.dot(hi, tri_ref[...], preferred_element_type=F32)
                  + jnp.dot(lo, tri_ref[...], preferred_element_type=F32))
        total = (jnp.dot(hi, ones_ref[...], preferred_element_type=F32)
                 + jnp.dot(lo, ones_ref[...], preferred_element_type=F32))
        r = r_ref[...]
        w = jnp.exp(z + suffix + r)
        if diagonal:
            w = jnp.where(causal, w, 0.0)
        acc_ref[...] += jnp.dot(w.astype(BF16), split_heads(v_ref[keys, :]), preferred_element_type=F32)
        r = r + total
        r_ref[...] = r
        return jnp.max(r)

    r_max = block(i, True)

    def more(c):
        j, r_max = c
        return jnp.logical_and(j >= 0, r_max + z_bound > EXP_FLOOR)

    def step(c):
        j, _ = c
        return j - 1, block(j, False)

    lax.while_loop(more, step, (i - 1, r_max))
    o_ref[...] = acc_ref[...].astype(o_ref.dtype)


def _attention(q, k, v, batch):
    n = q.shape[0]
    seq = n // batch
    tb = ATT_BLOCK
    nq = seq // tb
    pairs = D_MODEL // LANES
    idx = jnp.arange(2 * tb)
    same_head = (idx[:, None] // tb) == (idx[None, :] // tb)
    tri = (same_head & (idx[:, None] >= idx[None, :])).astype(BF16)
    ones = same_head.astype(BF16)
    kv_spec = pl.BlockSpec((seq, LANES), lambda b, h, i: (b, h))
    q_spec = pl.BlockSpec((tb, LANES), lambda b, h, i: (b * nq + i, h))
    return pl.pallas_call(
        _attn_kernel,
        grid=(batch, pairs, nq),
        in_specs=[q_spec, kv_spec, kv_spec, _resident(tri.shape), _resident(ones.shape)],
        out_specs=q_spec,
        out_shape=jax.ShapeDtypeStruct((n, D_MODEL), BF16),
        scratch_shapes=[pltpu.VMEM((tb, LANES), F32), pltpu.VMEM((tb, 2 * tb), F32),
                        pltpu.SMEM((1,), F32)],
        compiler_params=_params("parallel", "parallel", "arbitrary"),
        name="sb_attention",
    )(q, k, v, tri, ones)


def kernel(x, s5_w_in, s5_lambda_re, s5_lambda_im, s5_b_re, s5_b_im, s5_c_re, s5_c_im, s5_d, s5_log_step,
           s5_w_glu, s5_b_glu, s5_w_out, sb_w_kv, sb_w_q, sb_w_out, mlp_w1, mlp_b1, mlp_w2, mlp_b2,
           ln_mix_g, ln_mix_b, ln_mlp_g, ln_mlp_b):
    batch, seq, d = x.shape
    n = batch * seq
    rows = n // KB
    bf = lambda a: a.astype(BF16)

    t_op, w_in, v_out, pwr, pwi = _s5_operators(s5_lambda_re[0], s5_lambda_im[0], s5_log_step[0],
                                                s5_b_re[0], s5_b_im[0], s5_c_re[0], s5_c_im[0])
    d_tiled = jnp.tile(s5_d[0].reshape(N_TILES, 1, LANES), (1, 1, KB))
    xv = x.reshape(rows, KB * d)
    u3 = _u_proj(xv, bf(s5_w_in[0]))
    y3 = _ssm(u3, t_op, w_in, v_out, pwr, pwi, d_tiled, batch)
    tm = min(rows, 512)
    x1 = _layer_tail(
        y3, xv,
        pl.BlockSpec((None, tm, d), lambda i, j: (j, i, 0)),
        pl.BlockSpec((tm, d), lambda i, j: (i, j)),
        pl.BlockSpec((tm, d), lambda i, j: (i, j)),
        jax.ShapeDtypeStruct((rows, KB * d), F32),
        (rows // tm, KB),
        [bf(s5_w_glu[0]), s5_b_glu[0], bf(s5_w_out[0]), ln_mix_g[0], ln_mix_b[0],
         bf(mlp_w1[0]), mlp_b1[0], bf(mlp_w2[0]), mlp_b2[0], ln_mlp_g[0], ln_mlp_b[0]],
        glu=True)
    x1 = x1.reshape(n, d)

    q, k, v = _qkv(x1, bf(jnp.concatenate([sb_w_q[0], sb_w_kv], axis=1)))
    o = _attention(q, k, v, batch)
    tm = min(n, 512)
    row = pl.BlockSpec((tm, d), lambda i: (i, 0))
    out = _layer_tail(
        o, x1, row, row, row, jax.ShapeDtypeStruct((n, d), F32), (n // tm,),
        [bf(sb_w_out[0]), ln_mix_g[1], ln_mix_b[1],
         bf(mlp_w1[1]), mlp_b1[1], bf(mlp_w2[1]), mlp_b2[1], ln_mlp_g[1], ln_mlp_b[1]],
        glu=False)
    return out.reshape(batch, seq, d)
```

```python
import functools
import math

import jax
import jax.numpy as jnp
from jax import lax
from jax.experimental import pallas as pl
from jax.experimental.pallas import tpu as pltpu

F32 = jnp.float32
BF16 = jnp.bfloat16

D_MODEL = 1024
DEPTH = 2
S5_GROUP = 16
S5_GROUPS = D_MODEL // S5_GROUP
S5_STATE = 64
SB_HEADS = 16
SB_HEAD_DIM = D_MODEL // SB_HEADS
D_FF = 4 * D_MODEL
DEEPNORM_ALPHA = (2.0 * DEPTH) ** 0.25
LN_EPS = 1e-5

LANES = 128
SUBLANES = 8
KB = 16
GROUPS_PER_TILE = LANES // S5_GROUP
N_TILES = D_MODEL // LANES
TILE_STATE = GROUPS_PER_TILE * S5_STATE
ALL_STATE = S5_GROUPS * S5_STATE
VMEM_LIMIT = 56 * 1024 * 1024
ATT_BLOCK = 128
ATT_PAIRS = 4
EXP2_FLOOR = -130.0
LOG2_E = 1.0 / math.log(2.0)
NT_DIMS = (((1,), (1,)), ((), ()))


def _params(*sem):
    return pltpu.CompilerParams(dimension_semantics=sem, vmem_limit_bytes=VMEM_LIMIT)


def _resident(shape):
    zeros = (0,) * len(shape)
    return pl.BlockSpec(shape, lambda *_: zeros, pipeline_mode=pl.Buffered(1))


def _cmul(ar, ai, br, bi):
    return ar * br - ai * bi, ar * bi + ai * br


def _s5_prep_kernel(lr_ref, li_ref, ls_ref, br_ref, bi_ref, cr_ref, ci_ref,
                    car_ref, cai_ref, wr_ref, wi_ref, pwr_ref, pwi_ref,
                    a1r_ref, a1i_ref, pr_ref, pi_ref, bbr_ref, bbi_ref):
    tau = pl.program_id(0)

    @pl.when(tau == 0)
    def _():
        lr = lr_ref[...]
        li = li_ref[...]
        dt = jnp.exp(ls_ref[...])
        mag = jnp.exp(lr * dt)
        ang = li * dt
        a_re = mag * jnp.cos(ang)
        a_im = mag * jnp.sin(ang)
        nr = a_re - 1.0
        ni = a_im
        den = lr * lr + li * li
        f_re = (nr * lr + ni * li) / den
        f_im = (ni * lr - nr * li) / den
        bbr, bbi = _cmul(f_re, f_im, br_ref[...], bi_ref[...])
        bbr_ref[...] = bbr
        bbi_ref[...] = bbi
        a1r_ref[...] = a_re
        a1i_ref[...] = a_im
        pr_ref[...] = jnp.ones_like(a_re)
        pi_ref[...] = jnp.zeros_like(a_im)

    p_re = pr_ref[...]
    p_im = pi_ref[...]
    car, cai = _cmul(cr_ref[...], ci_ref[...], p_re, p_im)
    car_ref[...] = car
    cai_ref[...] = -cai
    wr, wi = _cmul(p_re, p_im, bbr_ref[...], bbi_ref[...])
    wr_ref[...] = wr
    wi_ref[...] = wi

    @pl.when(tau == KB)
    def _():
        q_re, q_im = p_re, p_im
        for i in range(SUBLANES):
            pwr_ref[i:i + 1, :] = q_re
            pwi_ref[i:i + 1, :] = q_im
            q_re, q_im = _cmul(q_re, q_im, p_re, p_im)

    n_re, n_im = _cmul(p_re, p_im, a1r_ref[...], a1i_ref[...])
    pr_ref[...] = n_re
    pi_ref[...] = n_im


def _s5_prep(lam_re, lam_im, log_step, b_re, b_im, c_re, c_im):
    g, p, h = S5_GROUPS, S5_STATE, S5_GROUP
    lr = lam_re.reshape(1, ALL_STATE)
    li = lam_im.reshape(1, ALL_STATE)
    ls = jnp.broadcast_to(log_step[:, None], (g, p)).reshape(1, ALL_STATE)
    br = b_re.transpose(2, 0, 1).reshape(h, ALL_STATE)
    bi = b_im.transpose(2, 0, 1).reshape(h, ALL_STATE)
    cr = c_re.transpose(1, 0, 2).reshape(h, ALL_STATE)
    ci = c_im.transpose(1, 0, 2).reshape(h, ALL_STATE)
    small = pl.BlockSpec((1, ALL_STATE), lambda t: (0, 0))
    big = pl.BlockSpec((h, ALL_STATE), lambda t: (0, 0))
    step = pl.BlockSpec((None, h, ALL_STATE), lambda t: (t, 0, 0))
    pw = pl.BlockSpec((SUBLANES, ALL_STATE), lambda t: (0, 0))
    out3 = jax.ShapeDtypeStruct((KB + 1, h, ALL_STATE), F32)
    outp = jax.ShapeDtypeStruct((SUBLANES, ALL_STATE), F32)
    return pl.pallas_call(
        _s5_prep_kernel,
        grid=(KB + 1,),
        in_specs=[small, small, small, big, big, big, big],
        out_specs=[step, step, step, step, pw, pw],
        out_shape=[out3, out3, out3, out3, outp, outp],
        scratch_shapes=[pltpu.VMEM((1, ALL_STATE), F32)] * 4 + [pltpu.VMEM((h, ALL_STATE), F32)] * 2,
        compiler_params=_params("arbitrary"),
        name="s5_prep",
    )(lr, li, ls, br, bi, cr, ci)


def _uproj_kernel(x_ref, w_ref, o_ref):
    o_ref[...] = jnp.dot(x_ref[...].astype(BF16), w_ref[...], preferred_element_type=F32)


def _u_proj(xv, w_in):
    rows = xv.shape[0]
    tm = min(rows, 512)
    return pl.pallas_call(
        _uproj_kernel,
        grid=(rows // tm, KB),
        in_specs=[pl.BlockSpec((tm, D_MODEL), lambda i, j: (i, j)), _resident((D_MODEL, D_MODEL))],
        out_specs=pl.BlockSpec((None, tm, D_MODEL), lambda i, j: (j, i, 0)),
        out_shape=jax.ShapeDtypeStruct((KB, rows, D_MODEL), F32),
        compiler_params=_params("parallel", "arbitrary"),
        name="s5_u_proj",
    )(xv, w_in)


def _shift_rows(x, k, row):
    return jnp.where(row >= k, pltpu.roll(x, k, 0), 0.0)


def _ssm_build_operators(car_ref, cai_ref, wr_ref, wi_ref, t_ref, win_ref, voutt_ref):
    ts = TILE_STATE
    row_group = lax.broadcasted_iota(jnp.int32, (LANES, ts), 0) // S5_GROUP
    lane_group = lax.broadcasted_iota(jnp.int32, (LANES, ts), 1) // S5_STATE
    same_group = row_group == lane_group

    def expand(re16, im16):
        tile = lambda a: jnp.where(same_group, jnp.concatenate([a] * GROUPS_PER_TILE, axis=0), 0.0)
        return jnp.concatenate([tile(re16), tile(im16)], axis=1)

    t_ref[...] = jnp.zeros_like(t_ref)
    bbar = expand(wr_ref[0], wi_ref[0])
    for tau in range(KB + 1):
        ca = expand(car_ref[tau], cai_ref[tau])
        if tau >= 1:
            voutt_ref[(tau - 1) * LANES:tau * LANES, :] = ca.astype(BF16)
        if tau < KB:
            lag = lax.dot_general(bbar, ca, NT_DIMS, precision=lax.Precision.HIGHEST,
                                  preferred_element_type=F32).astype(BF16)
            for j in range(KB - tau):
                t_ref[j * LANES:(j + 1) * LANES, (j + tau) * LANES:(j + tau + 1) * LANES] = lag
            win_ref[(KB - 1 - tau) * LANES:(KB - tau) * LANES, :] = expand(wr_ref[tau], wi_ref[tau]).astype(BF16)


def _ssm_kernel(u_ref, car_ref, cai_ref, wr_ref, wi_ref, pwr_ref, pwi_ref, d_ref, y_ref,
                s_ref, t_ref, win_ref, voutt_ref):
    nb = u_ref.shape[1]
    ts = TILE_STATE

    @pl.when(pl.program_id(1) == 0)
    def _():
        _ssm_build_operators(car_ref, cai_ref, wr_ref, wi_ref, t_ref, win_ref, voutt_ref)

    u_cat = jnp.concatenate([u_ref[j] for j in range(KB)], axis=1)
    u_bf = u_cat.astype(BF16)
    s_ref[...] = jnp.dot(u_bf, win_ref[...], preferred_element_type=F32)
    row = lax.broadcasted_iota(jnp.int32, (SUBLANES, ts), 0)
    pw_re = pwr_ref[...]
    pw_im = pwi_ref[...]

    def group(r, carry):
        c_re, c_im = carry
        rows = pl.ds(pl.multiple_of(r * SUBLANES, SUBLANES), SUBLANES)
        x_re = s_ref[rows, 0:ts]
        x_im = s_ref[rows, ts:2 * ts]
        for k in (1, 2, 4):
            a_re = pw_re[k - 1:k]
            a_im = pw_im[k - 1:k]
            sh_re = _shift_rows(x_re, k, row)
            sh_im = _shift_rows(x_im, k, row)
            x_re, x_im = x_re + a_re * sh_re - a_im * sh_im, x_im + a_re * sh_im + a_im * sh_re
        x_re, x_im = x_re + pw_re * c_re - pw_im * c_im, x_im + pw_re * c_im + pw_im * c_re
        s_ref[rows, 0:ts] = jnp.where(row >= 1, pltpu.roll(x_re, 1, 0), c_re)
        s_ref[rows, ts:2 * ts] = jnp.where(row >= 1, pltpu.roll(x_im, 1, 0), c_im)
        return x_re[SUBLANES - 1:SUBLANES], x_im[SUBLANES - 1:SUBLANES]

    zero = jnp.zeros((1, ts), F32)
    lax.fori_loop(0, nb // SUBLANES, group, (zero, zero))
    s_bf = s_ref[...].astype(BF16)
    width = 2 * LANES
    for c in range(KB * LANES // width):
        cols = slice(c * width, (c + 1) * width)
        live = (c + 1) * width
        y = jnp.dot(u_bf[:, :live], t_ref[0:live, cols], preferred_element_type=F32)
        y = y + lax.dot_general(s_bf, voutt_ref[cols, :], NT_DIMS, preferred_element_type=F32)
        y = y + d_ref[:, cols] * u_cat[:, cols]
        for jj in range(width // LANES):
            y_ref[c * (width // LANES) + jj] = y[:, jj * LANES:(jj + 1) * LANES]


def _ssm(u3, car, cai, wr, wi, pwr, pwi, d_tiled, batch):
    rows = u3.shape[1]
    nb = rows // batch
    factor = pl.BlockSpec((KB + 1, S5_GROUP, TILE_STATE), lambda c, b: (0, 0, c))
    power = pl.BlockSpec((SUBLANES, TILE_STATE), lambda c, b: (0, c))
    io = pl.BlockSpec((KB, nb, LANES), lambda c, b: (0, b, c))
    return pl.pallas_call(
        _ssm_kernel,
        grid=(N_TILES, batch),
        in_specs=[io, factor, factor, factor, factor, power, power,
                  pl.BlockSpec((None, 1, KB * LANES), lambda c, b: (c, 0, 0))],
        out_specs=io,
        out_shape=jax.ShapeDtypeStruct(u3.shape, F32),
        scratch_shapes=[pltpu.VMEM((nb, 2 * TILE_STATE), F32),
                        pltpu.VMEM((KB * LANES, KB * LANES), BF16),
                        pltpu.VMEM((KB * LANES, 2 * TILE_STATE), BF16),
                        pltpu.VMEM((KB * LANES, 2 * TILE_STATE), BF16)],
        compiler_params=_params("parallel", "arbitrary"),
        name="s5_ssm",
    )(u3, car, cai, wr, wi, pwr, pwi, d_tiled)


def _layer_norm(z, g, b):
    mu = jnp.mean(z, axis=-1, keepdims=True)
    zc = z - mu
    var = jnp.mean(zc * zc, axis=-1, keepdims=True)
    return zc * lax.rsqrt(var + LN_EPS) * g + b


def _tail_kernel(*refs, glu):
    if glu:
        (m_ref, x_ref, wglu_ref, bglu_ref, wout_ref, g1_ref, b1n_ref,
         w1_ref, b1_ref, w2_ref, b2_ref, g2_ref, b2n_ref, o_ref) = refs
        g = jax.nn.gelu(m_ref[...])
        gate = jnp.dot(g.astype(BF16), wglu_ref[...], preferred_element_type=F32) + bglu_ref[...]
        m = (g * jax.nn.sigmoid(gate)).astype(BF16)
    else:
        (m_ref, x_ref, wout_ref, g1_ref, b1n_ref,
         w1_ref, b1_ref, w2_ref, b2_ref, g2_ref, b2n_ref, o_ref) = refs
        m = m_ref[...]
    mix = jnp.dot(m, wout_ref[...], preferred_element_type=F32)
    x1 = _layer_norm(DEEPNORM_ALPHA * x_ref[...] + mix, g1_ref[...], b1n_ref[...])
    x1_bf = x1.astype(BF16)
    ff = jnp.zeros_like(x1)
    for c in range(D_FF // D_MODEL):
        cols = slice(c * D_MODEL, (c + 1) * D_MODEL)
        hid = jnp.dot(x1_bf, w1_ref[:, cols], preferred_element_type=F32) + b1_ref[:, cols]
        hid = jnp.square(jnp.maximum(hid, 0.0))
        ff = ff + jnp.dot(hid.astype(BF16), w2_ref[cols, :], preferred_element_type=F32)
    ff = ff + b2_ref[...]
    o_ref[...] = _layer_norm(DEEPNORM_ALPHA * x1 + ff, g2_ref[...], b2n_ref[...])


def _layer_tail(m, x, m_spec, x_spec, out_spec, out_shape, grid, weights, glu):
    vec = lambda a: a.reshape(1, -1)
    ops = [m, x]
    specs = [m_spec, x_spec]
    for wgt in weights:
        a = vec(wgt) if wgt.ndim == 1 else wgt
        ops.append(a)
        specs.append(_resident(a.shape))
    return pl.pallas_call(
        functools.partial(_tail_kernel, glu=glu),
        grid=grid,
        in_specs=specs,
        out_specs=out_spec,
        out_shape=out_shape,
        compiler_params=_params(*(("parallel",) * len(grid))),
        name="layer_tail_glu" if glu else "layer_tail",
    )(*ops)


def _qkv_kernel(x_ref, w_ref, q_ref, k_ref, v_ref):
    y = jnp.dot(x_ref[...].astype(BF16), w_ref[...], preferred_element_type=F32)
    q_ref[...] = (y[:, :D_MODEL] * (LOG2_E / math.sqrt(SB_HEAD_DIM))).astype(BF16)
    k_ref[...] = y[:, D_MODEL:2 * D_MODEL].astype(BF16)
    v_ref[...] = y[:, 2 * D_MODEL:].astype(BF16)


def _qkv(x, w_qkv):
    n = x.shape[0]
    tm = min(n, 512)
    row = pl.BlockSpec((tm, D_MODEL), lambda i: (i, 0))
    out = jax.ShapeDtypeStruct((n, D_MODEL), BF16)
    return pl.pallas_call(
        _qkv_kernel,
        grid=(n // tm,),
        in_specs=[row, _resident(w_qkv.shape)],
        out_specs=[row, row, row],
        out_shape=[out, out, out],
        compiler_params=_params("parallel"),
        name="qkv_proj",
    )(x, w_qkv)


def _attn_kernel(q_ref, k_ref, v_ref, ntri_ref, hsum_ref, o_ref, acc_ref, r_ref, kn_ref):
    i = pl.program_id(2)
    tb = ATT_BLOCK
    pairs = q_ref.shape[1] // LANES
    lane = lax.broadcasted_iota(jnp.int32, (tb, LANES), 1)
    first = lane < SB_HEAD_DIM

    def split_heads(a):
        zero = jnp.zeros_like(a)
        return jnp.concatenate([jnp.where(first, a, zero), jnp.where(first, zero, a)], axis=0)

    def head_sq_norms(a):
        sq = jnp.square(a.astype(F32)).astype(BF16)
        return jnp.dot(sq, hsum_ref[...], preferred_element_type=F32)

    def pair_max_sq_norm(ref, rows):
        m = head_sq_norms(ref[rows, 0:LANES])
        for p in range(1, pairs):
            m = jnp.maximum(m, head_sq_norms(ref[rows, p * LANES:(p + 1) * LANES]))
        return m

    @pl.when(i == 0)
    def _():
        def part(c, m):
            rows = pl.ds(pl.multiple_of(c * tb, tb), tb)
            return jnp.maximum(m, pair_max_sq_norm(k_ref, rows))
        kn_ref[0] = jnp.max(lax.fori_loop(0, k_ref.shape[0] // tb, part, jnp.zeros((tb, LANES), F32)))

    qn = jnp.max(pair_max_sq_norm(q_ref, slice(None)))
    z_bound = jnp.sqrt(qn * kn_ref[0]) * 1.02 + 1e-3
    acc_ref[...] = jnp.zeros_like(acc_ref)
    r_ref[...] = jnp.zeros_like(r_ref)
    rr = lax.broadcasted_iota(jnp.int32, (tb, 2 * tb), 0)
    cc = lax.broadcasted_iota(jnp.int32, (tb, 2 * tb), 1)
    causal = jnp.bitwise_and(cc, tb - 1) < rr
    sign_bit = jnp.int32(-2 ** 31)

    def block(j, diagonal):
        keys = pl.ds(pl.multiple_of(j * tb, tb), tb)
        lanes = [slice(p * LANES, (p + 1) * LANES) for p in range(pairs)]
        z = [lax.dot_general(q_ref[:, lanes[p]], split_heads(k_ref[keys, lanes[p]]), NT_DIMS,
                             preferred_element_type=F32) for p in range(pairs)]
        cost, suffix = [], []
        for p in range(pairs):
            neg_abs = pltpu.bitcast(pltpu.bitcast(z[p], jnp.int32) | sign_bit, F32)
            c = jnp.maximum(z[p], 0.0) + jnp.log2(1.0 + jnp.exp2(neg_abs))
            cost.append(jnp.where(causal, c, 0.0) if diagonal else c)
        for p in range(pairs):
            hi = cost[p].astype(BF16)
            lo = (cost[p] - hi.astype(F32)).astype(BF16)
            suffix.append(jnp.dot(hi, ntri_ref[...], preferred_element_type=F32)
                          + jnp.dot(lo, ntri_ref[...], preferred_element_type=F32))
        r_top = None
        for p in range(pairs):
            r = r_ref[p]
            w = jnp.exp2(z[p] + suffix[p] + r)
            if diagonal:
                w = jnp.where(causal, w, 0.0)
            acc_ref[:, lanes[p]] += jnp.dot(w.astype(BF16), split_heads(v_ref[keys, lanes[p]]),
                                            preferred_element_type=F32)
            spent = [jnp.broadcast_to(jnp.sum(cost[p][:, h * tb:(h + 1) * tb], axis=1, keepdims=True), (tb, tb))
                     for h in range(2)]
            r = r - jnp.concatenate(spent, axis=1)
            r_ref[p] = r
            r_top = r if r_top is None else jnp.maximum(r_top, r)
        return jnp.max(r_top)

    r_max = block(i, True)

    def more(c):
        j, r_max = c
        return jnp.logical_and(j >= 0, r_max + z_bound > EXP2_FLOOR)

    def step(c):
        j, _ = c
        return j - 1, block(j, False)

    lax.while_loop(more, step, (i - 1, r_max))
    o_ref[...] = acc_ref[...].astype(o_ref.dtype)


def _attention(q, k, v, batch):
    n = q.shape[0]
    seq = n // batch
    tb = ATT_BLOCK
    nq = seq // tb
    width = ATT_PAIRS * LANES
    idx = jnp.arange(2 * tb)
    same_head = (idx[:, None] // tb) == (idx[None, :] // tb)
    ntri = -(same_head & (idx[:, None] >= idx[None, :])).astype(BF16)
    hd = jnp.arange(LANES) // SB_HEAD_DIM
    hsum = (hd[:, None] == hd[None, :]).astype(BF16)
    kv_spec = pl.BlockSpec((seq, width), lambda b, h, i: (b, h))
    q_spec = pl.BlockSpec((tb, width), lambda b, h, i: (b * nq + i, h))
    return pl.pallas_call(
        _attn_kernel,
        grid=(batch, D_MODEL // width, nq),
        in_specs=[q_spec, kv_spec, kv_spec, _resident(ntri.shape), _resident(hsum.shape)],
        out_specs=q_spec,
        out_shape=jax.ShapeDtypeStruct((n, D_MODEL), BF16),
        scratch_shapes=[pltpu.VMEM((tb, width), F32), pltpu.VMEM((ATT_PAIRS, tb, 2 * tb), F32),
                        pltpu.SMEM((1,), F32)],
        compiler_params=_params("parallel", "parallel", "arbitrary"),
        name="sb_attention",
    )(q, k, v, ntri, hsum)


def kernel(x, s5_w_in, s5_lambda_re, s5_lambda_im, s5_b_re, s5_b_im, s5_c_re, s5_c_im, s5_d, s5_log_step,
           s5_w_glu, s5_b_glu, s5_w_out, sb_w_kv, sb_w_q, sb_w_out, mlp_w1, mlp_b1, mlp_w2, mlp_b2,
           ln_mix_g, ln_mix_b, ln_mlp_g, ln_mlp_b):
    batch, seq, d = x.shape
    n = batch * seq
    rows = n // KB
    bf = lambda a: a.astype(BF16)

    car, cai, wr, wi, pwr, pwi = _s5_prep(s5_lambda_re[0], s5_lambda_im[0], s5_log_step[0],
                                          s5_b_re[0], s5_b_im[0], s5_c_re[0], s5_c_im[0])
    d_tiled = jnp.tile(s5_d[0].reshape(N_TILES, 1, LANES), (1, 1, KB))
    xv = x.reshape(rows, KB * d)
    u3 = _u_proj(xv, bf(s5_w_in[0]))
    y3 = _ssm(u3, car, cai, wr, wi, pwr, pwi, d_tiled, batch)
    tm = min(rows, 512)
    x1 = _layer_tail(
        y3, xv,
        pl.BlockSpec((None, tm, d), lambda i, j: (j, i, 0)),
        pl.BlockSpec((tm, d), lambda i, j: (i, j)),
        pl.BlockSpec((tm, d), lambda i, j: (i, j)),
        jax.ShapeDtypeStruct((rows, KB * d), F32),
        (rows // tm, KB),
        [bf(s5_w_glu[0]), s5_b_glu[0], bf(s5_w_out[0]), ln_mix_g[0], ln_mix_b[0],
         bf(mlp_w1[0]), mlp_b1[0], bf(mlp_w2[0]), mlp_b2[0], ln_mlp_g[0], ln_mlp_b[0]],
        glu=True)
    x1 = x1.reshape(n, d)

    q, k, v = _qkv(x1, bf(jnp.concatenate([sb_w_q[0], sb_w_kv], axis=1)))
    o = _attention(q, k, v, batch)
    tm = min(n, 512)
    row = pl.BlockSpec((tm, d), lambda i: (i, 0))
    out = _layer_tail(
        o, x1, row, row, row, jax.ShapeDtypeStruct((n, d), F32), (n // tm,),
        [bf(sb_w_out[0]), ln_mix_g[1], ln_mix_b[1],
         bf(mlp_w1[1]), mlp_b1[1], bf(mlp_w2[1]), mlp_b2[1], ln_mlp_g[1], ln_mlp_b[1]],
        glu=False)
    return out.reshape(batch, seq, d)
```

```python
import functools
import math

import jax
import jax.numpy as jnp
from jax import lax
from jax.experimental import pallas as pl
from jax.experimental.pallas import tpu as pltpu

F32 = jnp.float32
BF16 = jnp.bfloat16

D_MODEL = 1024
DEPTH = 2
S5_GROUP = 16
S5_GROUPS = D_MODEL // S5_GROUP
S5_STATE = 64
SB_HEADS = 16
SB_HEAD_DIM = D_MODEL // SB_HEADS
D_FF = 4 * D_MODEL
DEEPNORM_ALPHA = (2.0 * DEPTH) ** 0.25
LN_EPS = 1e-5

LANES = 128
SUBLANES = 8
KB = 16
GROUPS_PER_TILE = LANES // S5_GROUP
N_TILES = D_MODEL // LANES
TILE_STATE = GROUPS_PER_TILE * S5_STATE
ALL_STATE = S5_GROUPS * S5_STATE
VMEM_LIMIT = 56 * 1024 * 1024
ATT_BLOCK = 128
ATT_PAIRS = 4
FIRST_VISIT_BLOCKS = 3
EXP2_FLOOR = -130.0
LOG2_E = 1.0 / math.log(2.0)
NT_DIMS = (((1,), (1,)), ((), ()))


def _params(*sem):
    return pltpu.CompilerParams(dimension_semantics=sem, vmem_limit_bytes=VMEM_LIMIT)


def _resident(shape):
    zeros = (0,) * len(shape)
    return pl.BlockSpec(shape, lambda *_: zeros, pipeline_mode=pl.Buffered(1))


def _cmul(ar, ai, br, bi):
    return ar * br - ai * bi, ar * bi + ai * br


def _s5_prep_kernel(lr_ref, li_ref, ls_ref, br_ref, bi_ref, cr_ref, ci_ref,
                    car_ref, cai_ref, wr_ref, wi_ref, pwr_ref, pwi_ref,
                    a1r_ref, a1i_ref, pr_ref, pi_ref, bbr_ref, bbi_ref):
    tau = pl.program_id(0)

    @pl.when(tau == 0)
    def _():
        lr = lr_ref[...]
        li = li_ref[...]
        dt = jnp.exp(ls_ref[...])
        mag = jnp.exp(lr * dt)
        ang = li * dt
        a_re = mag * jnp.cos(ang)
        a_im = mag * jnp.sin(ang)
        nr = a_re - 1.0
        ni = a_im
        den = lr * lr + li * li
        f_re = (nr * lr + ni * li) / den
        f_im = (ni * lr - nr * li) / den
        bbr, bbi = _cmul(f_re, f_im, br_ref[...], bi_ref[...])
        bbr_ref[...] = bbr
        bbi_ref[...] = bbi
        a1r_ref[...] = a_re
        a1i_ref[...] = a_im
        pr_ref[...] = jnp.ones_like(a_re)
        pi_ref[...] = jnp.zeros_like(a_im)

    p_re = pr_ref[...]
    p_im = pi_ref[...]
    car, cai = _cmul(cr_ref[...], ci_ref[...], p_re, p_im)
    car_ref[...] = car
    cai_ref[...] = -cai
    wr, wi = _cmul(p_re, p_im, bbr_ref[...], bbi_ref[...])
    wr_ref[...] = wr
    wi_ref[...] = wi

    @pl.when(tau == KB)
    def _():
        q_re, q_im = p_re, p_im
        for i in range(SUBLANES):
            pwr_ref[i:i + 1, :] = q_re
            pwi_ref[i:i + 1, :] = q_im
            q_re, q_im = _cmul(q_re, q_im, p_re, p_im)

    n_re, n_im = _cmul(p_re, p_im, a1r_ref[...], a1i_ref[...])
    pr_ref[...] = n_re
    pi_ref[...] = n_im


def _s5_prep(lam_re, lam_im, log_step, b_re, b_im, c_re, c_im):
    g, p, h = S5_GROUPS, S5_STATE, S5_GROUP
    lr = lam_re.reshape(1, ALL_STATE)
    li = lam_im.reshape(1, ALL_STATE)
    ls = jnp.broadcast_to(log_step[:, None], (g, p)).reshape(1, ALL_STATE)
    br = b_re.transpose(2, 0, 1).reshape(h, ALL_STATE)
    bi = b_im.transpose(2, 0, 1).reshape(h, ALL_STATE)
    cr = c_re.transpose(1, 0, 2).reshape(h, ALL_STATE)
    ci = c_im.transpose(1, 0, 2).reshape(h, ALL_STATE)
    small = pl.BlockSpec((1, ALL_STATE), lambda t: (0, 0))
    big = pl.BlockSpec((h, ALL_STATE), lambda t: (0, 0))
    step = pl.BlockSpec((None, h, ALL_STATE), lambda t: (t, 0, 0))
    pw = pl.BlockSpec((SUBLANES, ALL_STATE), lambda t: (0, 0))
    out3 = jax.ShapeDtypeStruct((KB + 1, h, ALL_STATE), F32)
    outp = jax.ShapeDtypeStruct((SUBLANES, ALL_STATE), F32)
    return pl.pallas_call(
        _s5_prep_kernel,
        grid=(KB + 1,),
        in_specs=[small, small, small, big, big, big, big],
        out_specs=[step, step, step, step, pw, pw],
        out_shape=[out3, out3, out3, out3, outp, outp],
        scratch_shapes=[pltpu.VMEM((1, ALL_STATE), F32)] * 4 + [pltpu.VMEM((h, ALL_STATE), F32)] * 2,
        compiler_params=_params("arbitrary"),
        name="s5_prep",
    )(lr, li, ls, br, bi, cr, ci)


def _uproj_kernel(x_ref, w_ref, o_ref):
    o_ref[...] = jnp.dot(x_ref[...].astype(BF16), w_ref[...], preferred_element_type=F32)


def _u_proj(x, w_in):
    n = x.shape[0]
    tm = min(n, 1024)
    row = pl.BlockSpec((tm, D_MODEL), lambda i: (i, 0))
    return pl.pallas_call(
        _uproj_kernel,
        grid=(n // tm,),
        in_specs=[row, _resident((D_MODEL, D_MODEL))],
        out_specs=row,
        out_shape=jax.ShapeDtypeStruct((n, D_MODEL), F32),
        compiler_params=_params("parallel"),
        name="s5_u_proj",
    )(x, w_in)


def _shift_rows(x, k, row):
    return jnp.where(row >= k, pltpu.roll(x, k, 0), 0.0)


def _ssm_build_operators(car_ref, cai_ref, wr_ref, wi_ref, t_ref, win_ref, voutt_ref):
    ts = TILE_STATE
    row_group = lax.broadcasted_iota(jnp.int32, (LANES, ts), 0) // S5_GROUP
    lane_group = lax.broadcasted_iota(jnp.int32, (LANES, ts), 1) // S5_STATE
    same_group = row_group == lane_group

    def expand(re16, im16):
        tile = lambda a: jnp.where(same_group, jnp.concatenate([a] * GROUPS_PER_TILE, axis=0), 0.0)
        return jnp.concatenate([tile(re16), tile(im16)], axis=1)

    t_ref[...] = jnp.zeros_like(t_ref)
    bbar = expand(wr_ref[0], wi_ref[0])
    for tau in range(KB + 1):
        ca = expand(car_ref[tau], cai_ref[tau])
        if tau >= 1:
            voutt_ref[(tau - 1) * LANES:tau * LANES, :] = ca.astype(BF16)
        if tau < KB:
            lag = lax.dot_general(bbar, ca, NT_DIMS, precision=lax.Precision.HIGHEST,
                                  preferred_element_type=F32).astype(BF16)
            for j in range(KB - tau):
                t_ref[j * LANES:(j + 1) * LANES, (j + tau) * LANES:(j + tau + 1) * LANES] = lag
            win_ref[(KB - 1 - tau) * LANES:(KB - tau) * LANES, :] = expand(wr_ref[tau], wi_ref[tau]).astype(BF16)


def _ssm_kernel(u_ref, car_ref, cai_ref, wr_ref, wi_ref, pwr_ref, pwi_ref, d_ref, y_ref,
                s_ref, t_ref, win_ref, voutt_ref):
    nb = u_ref.shape[0] // KB
    ts = TILE_STATE

    @pl.when(pl.program_id(1) == 0)
    def _():
        _ssm_build_operators(car_ref, cai_ref, wr_ref, wi_ref, t_ref, win_ref, voutt_ref)

    u_cat = jnp.concatenate([u_ref[pl.ds(j, nb, stride=KB), :] for j in range(KB)], axis=1)
    u_bf = u_cat.astype(BF16)
    s_ref[...] = jnp.dot(u_bf, win_ref[...], preferred_element_type=F32)
    row = lax.broadcasted_iota(jnp.int32, (SUBLANES, ts), 0)
    pw_re = pwr_ref[...]
    pw_im = pwi_ref[...]

    def group(r, carry):
        c_re, c_im = carry
        rows = pl.ds(pl.multiple_of(r * SUBLANES, SUBLANES), SUBLANES)
        x_re = s_ref[rows, 0:ts]
        x_im = s_ref[rows, ts:2 * ts]
        for k in (1, 2, 4):
            a_re = pw_re[k - 1:k]
            a_im = pw_im[k - 1:k]
            sh_re = _shift_rows(x_re, k, row)
            sh_im = _shift_rows(x_im, k, row)
            x_re, x_im = x_re + a_re * sh_re - a_im * sh_im, x_im + a_re * sh_im + a_im * sh_re
        x_re, x_im = x_re + pw_re * c_re - pw_im * c_im, x_im + pw_re * c_im + pw_im * c_re
        s_ref[rows, 0:ts] = jnp.where(row >= 1, pltpu.roll(x_re, 1, 0), c_re)
        s_ref[rows, ts:2 * ts] = jnp.where(row >= 1, pltpu.roll(x_im, 1, 0), c_im)
        return x_re[SUBLANES - 1:SUBLANES], x_im[SUBLANES - 1:SUBLANES]

    zero = jnp.zeros((1, ts), F32)
    lax.fori_loop(0, nb // SUBLANES, group, (zero, zero))
    s_bf = s_ref[...].astype(BF16)
    width = 2 * LANES
    for c in range(KB * LANES // width):
        cols = slice(c * width, (c + 1) * width)
        live = (c + 1) * width
        y = jnp.dot(u_bf[:, :live], t_ref[0:live, cols], preferred_element_type=F32)
        y = y + lax.dot_general(s_bf, voutt_ref[cols, :], NT_DIMS, preferred_element_type=F32)
        y = y + d_ref[:, cols] * u_cat[:, cols]
        for jj in range(width // LANES):
            y_ref[pl.ds(c * (width // LANES) + jj, nb, stride=KB), :] = y[:, jj * LANES:(jj + 1) * LANES]


def _ssm(u, car, cai, wr, wi, pwr, pwi, d_tiled, batch):
    seq = u.shape[0] // batch
    nb = seq // KB
    factor = pl.BlockSpec((KB + 1, S5_GROUP, TILE_STATE), lambda c, b: (0, 0, c))
    power = pl.BlockSpec((SUBLANES, TILE_STATE), lambda c, b: (0, c))
    io = pl.BlockSpec((seq, LANES), lambda c, b: (b, c))
    return pl.pallas_call(
        _ssm_kernel,
        grid=(N_TILES, batch),
        in_specs=[io, factor, factor, factor, factor, power, power,
                  pl.BlockSpec((None, 1, KB * LANES), lambda c, b: (c, 0, 0))],
        out_specs=io,
        out_shape=jax.ShapeDtypeStruct(u.shape, F32),
        scratch_shapes=[pltpu.VMEM((nb, 2 * TILE_STATE), F32),
                        pltpu.VMEM((KB * LANES, KB * LANES), BF16),
                        pltpu.VMEM((KB * LANES, 2 * TILE_STATE), BF16),
                        pltpu.VMEM((KB * LANES, 2 * TILE_STATE), BF16)],
        compiler_params=_params("parallel", "arbitrary"),
        name="s5_ssm",
    )(u, car, cai, wr, wi, pwr, pwi, d_tiled)


def _layer_norm(z, g, b):
    mu = jnp.mean(z, axis=-1, keepdims=True)
    zc = z - mu
    var = jnp.mean(zc * zc, axis=-1, keepdims=True)
    return zc * lax.rsqrt(var + LN_EPS) * g + b


def _tail_kernel(*refs, glu):
    if glu:
        (m_ref, x_ref, wglu_ref, bglu_ref, wout_ref, g1_ref, b1n_ref,
         w1_ref, b1_ref, w2_ref, b2_ref, g2_ref, b2n_ref, o_ref) = refs
        g = jax.nn.gelu(m_ref[...])
        gate = jnp.dot(g.astype(BF16), wglu_ref[...], preferred_element_type=F32) + bglu_ref[...]
        m = (g * jax.nn.sigmoid(gate)).astype(BF16)
    else:
        (m_ref, x_ref, wout_ref, g1_ref, b1n_ref,
         w1_ref, b1_ref, w2_ref, b2_ref, g2_ref, b2n_ref, o_ref) = refs
        m = m_ref[...]
    mix = jnp.dot(m, wout_ref[...], preferred_element_type=F32)
    x1 = _layer_norm(DEEPNORM_ALPHA * x_ref[...] + mix, g1_ref[...], b1n_ref[...])
    x1_bf = x1.astype(BF16)
    ff = jnp.zeros_like(x1)
    for c in range(D_FF // D_MODEL):
        cols = slice(c * D_MODEL, (c + 1) * D_MODEL)
        hid = jnp.dot(x1_bf, w1_ref[:, cols], preferred_element_type=F32) + b1_ref[:, cols]
        hid = jnp.square(jnp.maximum(hid, 0.0))
        ff = ff + jnp.dot(hid.astype(BF16), w2_ref[cols, :], preferred_element_type=F32)
    ff = ff + b2_ref[...]
    o_ref[...] = _layer_norm(DEEPNORM_ALPHA * x1 + ff, g2_ref[...], b2n_ref[...])


def _layer_tail(m, x, weights, glu):
    n = x.shape[0]
    tm = min(n, 512)
    row = pl.BlockSpec((tm, D_MODEL), lambda i: (i, 0))
    ops = [m, x]
    specs = [row, row]
    for wgt in weights:
        a = wgt.reshape(1, -1) if wgt.ndim == 1 else wgt
        ops.append(a)
        specs.append(_resident(a.shape))
    return pl.pallas_call(
        functools.partial(_tail_kernel, glu=glu),
        grid=(n // tm,),
        in_specs=specs,
        out_specs=row,
        out_shape=jax.ShapeDtypeStruct((n, D_MODEL), F32),
        compiler_params=_params("parallel"),
        name="layer_tail_glu" if glu else "layer_tail",
    )(*ops)


def _qkv_kernel(x_ref, w_ref, q_ref, k_ref, v_ref):
    y = jnp.dot(x_ref[...].astype(BF16), w_ref[...], preferred_element_type=F32)
    q_ref[...] = (y[:, :D_MODEL] * (LOG2_E / math.sqrt(SB_HEAD_DIM))).astype(BF16)
    k_ref[...] = y[:, D_MODEL:2 * D_MODEL].astype(BF16)
    v_ref[...] = y[:, 2 * D_MODEL:].astype(BF16)


def _qkv(x, w_qkv):
    n = x.shape[0]
    tm = min(n, 512)
    row = pl.BlockSpec((tm, D_MODEL), lambda i: (i, 0))
    out = jax.ShapeDtypeStruct((n, D_MODEL), BF16)
    return pl.pallas_call(
        _qkv_kernel,
        grid=(n // tm,),
        in_specs=[row, _resident(w_qkv.shape)],
        out_specs=[row, row, row],
        out_shape=[out, out, out],
        compiler_params=_params("parallel"),
        name="qkv_proj",
    )(x, w_qkv)


def _attn_kernel(q_ref, k_ref, v_ref, ntri_ref, hsum_ref, o_ref, acc_ref, r_ref, st_ref):
    i = pl.program_id(2)
    tb = ATT_BLOCK
    pairs = q_ref.shape[1] // LANES
    lane = lax.broadcasted_iota(jnp.int32, (tb, LANES), 1)
    first = lane < SB_HEAD_DIM

    def split_heads(a):
        zero = jnp.zeros_like(a)
        return jnp.concatenate([jnp.where(first, a, zero), jnp.where(first, zero, a)], axis=0)

    def head_sq_norms(a):
        sq = jnp.square(a.astype(F32)).astype(BF16)
        return jnp.dot(sq, hsum_ref[...], preferred_element_type=F32)

    def pair_max_sq_norm(ref, rows):
        m = head_sq_norms(ref[rows, 0:LANES])
        for p in range(1, pairs):
            m = jnp.maximum(m, head_sq_norms(ref[rows, p * LANES:(p + 1) * LANES]))
        return m

    @pl.when(i == 0)
    def _():
        def part(c, m):
            rows = pl.ds(pl.multiple_of(c * tb, tb), tb)
            return jnp.maximum(m, pair_max_sq_norm(k_ref, rows))
        st_ref[0] = jnp.max(lax.fori_loop(0, k_ref.shape[0] // tb, part, jnp.zeros((tb, LANES), F32)))

    acc_ref[...] = jnp.zeros_like(acc_ref)
    r_ref[...] = jnp.zeros_like(r_ref)
    rr = lax.broadcasted_iota(jnp.int32, (tb, 2 * tb), 0)
    cc = lax.broadcasted_iota(jnp.int32, (tb, 2 * tb), 1)
    causal = jnp.bitwise_and(cc, tb - 1) < rr
    lanes = [slice(p * LANES, (p + 1) * LANES) for p in range(pairs)]

    def visit(blocks, diagonal):
        keys = [pl.ds(pl.multiple_of(j * tb, tb), tb) for j in blocks]
        todo = [(p, t) for p in range(pairs) for t in range(len(blocks))]
        z = {pt: lax.dot_general(q_ref[:, lanes[pt[0]]], split_heads(k_ref[keys[pt[1]], lanes[pt[0]]]), NT_DIMS,
                                 preferred_element_type=F32) for pt in todo}
        cost = {}
        for pt in todo:
            c = jnp.maximum(z[pt], 0.0) + jnp.log2(1.0 + 1.0 / jnp.exp2(jnp.abs(z[pt])))
            cost[pt] = jnp.where(causal, c, 0.0) if (diagonal and pt[1] == 0) else c
        suffix = {}
        for pt in todo:
            hi = cost[pt].astype(BF16)
            lo = (cost[pt] - hi.astype(F32)).astype(BF16)
            suffix[pt] = (jnp.dot(hi, ntri_ref[...], preferred_element_type=F32)
                          + jnp.dot(lo, ntri_ref[...], preferred_element_type=F32))
        r_top = None
        for p in range(pairs):
            r = r_ref[p]
            weights = []
            for t in range(len(blocks)):
                w = jnp.exp2(z[p, t] + suffix[p, t] + r)
                if diagonal and t == 0:
                    w = jnp.where(causal, w, 0.0)
                weights.append(w.astype(BF16))
                spent = [jnp.broadcast_to(jnp.sum(cost[p, t][:, h * tb:(h + 1) * tb], axis=1, keepdims=True),
                                          (tb, tb)) for h in range(2)]
                r = r - jnp.concatenate(spent, axis=1)
            values = jnp.concatenate([split_heads(v_ref[keys[t], lanes[p]]) for t in range(len(blocks))], axis=0)
            acc_ref[:, lanes[p]] += jnp.dot(jnp.concatenate(weights, axis=1), values, preferred_element_type=F32)
            r_ref[p] = r
            r_top = r if r_top is None else jnp.maximum(r_top, r)
        return jnp.max(r_top)

    head_start = FIRST_VISIT_BLOCKS - 1

    @pl.when(i >= head_start)
    def _():
        st_ref[1] = visit([i - t for t in range(FIRST_VISIT_BLOCKS)], True)

    @pl.when(i < head_start)
    def _():
        st_ref[1] = visit([i], True)

    qn = jnp.max(pair_max_sq_norm(q_ref, slice(None)))
    z_bound = jnp.sqrt(qn * st_ref[0]) * 1.02 + 1e-3

    def more(c):
        j, r_max = c
        return jnp.logical_and(j >= 0, r_max + z_bound > EXP2_FLOOR)

    def step(c):
        j, _ = c
        return j - 1, visit([j], False)

    lax.while_loop(more, step, (jnp.where(i >= head_start, i - FIRST_VISIT_BLOCKS, i - 1), st_ref[1]))
    o_ref[...] = acc_ref[...].astype(o_ref.dtype)


def _attention(q, k, v, batch):
    n = q.shape[0]
    seq = n // batch
    tb = ATT_BLOCK
    nq = seq // tb
    width = ATT_PAIRS * LANES
    idx = jnp.arange(2 * tb)
    same_head = (idx[:, None] // tb) == (idx[None, :] // tb)
    ntri = -(same_head & (idx[:, None] >= idx[None, :])).astype(BF16)
    hd = jnp.arange(LANES) // SB_HEAD_DIM
    hsum = (hd[:, None] == hd[None, :]).astype(BF16)
    kv_spec = pl.BlockSpec((seq, width), lambda b, h, i: (b, h))
    q_spec = pl.BlockSpec((tb, width), lambda b, h, i: (b * nq + i, h))
    return pl.pallas_call(
        _attn_kernel,
        grid=(batch, D_MODEL // width, nq),
        in_specs=[q_spec, kv_spec, kv_spec, _resident(ntri.shape), _resident(hsum.shape)],
        out_specs=q_spec,
        out_shape=jax.ShapeDtypeStruct((n, D_MODEL), BF16),
        scratch_shapes=[pltpu.VMEM((tb, width), F32), pltpu.VMEM((ATT_PAIRS, tb, 2 * tb), F32),
                        pltpu.SMEM((2,), F32)],
        compiler_params=_params("parallel", "parallel", "arbitrary"),
        name="sb_attention",
    )(q, k, v, ntri, hsum)


def kernel(x, s5_w_in, s5_lambda_re, s5_lambda_im, s5_b_re, s5_b_im, s5_c_re, s5_c_im, s5_d, s5_log_step,
           s5_w_glu, s5_b_glu, s5_w_out, sb_w_kv, sb_w_q, sb_w_out, mlp_w1, mlp_b1, mlp_w2, mlp_b2,
           ln_mix_g, ln_mix_b, ln_mlp_g, ln_mlp_b):
    batch, seq, d = x.shape
    n = batch * seq
    bf = lambda a: a.astype(BF16)
    x0 = x.reshape(n, d)

    car, cai, wr, wi, pwr, pwi = _s5_prep(s5_lambda_re[0], s5_lambda_im[0], s5_log_step[0],
                                          s5_b_re[0], s5_b_im[0], s5_c_re[0], s5_c_im[0])
    d_tiled = jnp.tile(s5_d[0].reshape(N_TILES, 1, LANES), (1, 1, KB))
    u = _u_proj(x0, bf(s5_w_in[0]))
    y = _ssm(u, car, cai, wr, wi, pwr, pwi, d_tiled, batch)
    x1 = _layer_tail(
        y, x0,
        [bf(s5_w_glu[0]), s5_b_glu[0], bf(s5_w_out[0]), ln_mix_g[0], ln_mix_b[0],
         bf(mlp_w1[0]), mlp_b1[0], bf(mlp_w2[0]), mlp_b2[0], ln_mlp_g[0], ln_mlp_b[0]],
        glu=True)

    q, k, v = _qkv(x1, bf(jnp.concatenate([sb_w_q[0], sb_w_kv], axis=1)))
    o = _attention(q, k, v, batch)
    out = _layer_tail(
        o, x1,
        [bf(sb_w_out[0]), ln_mix_g[1], ln_mix_b[1],
         bf(mlp_w1[1]), mlp_b1[1], bf(mlp_w2[1]), mlp_b2[1], ln_mlp_g[1], ln_mlp_b[1]],
        glu=False)
    return out.reshape(batch, seq, d)
```

```python
import functools
import math

import jax
import jax.numpy as jnp
from jax import lax
from jax.experimental import pallas as pl
from jax.experimental.pallas import tpu as pltpu

F32 = jnp.float32
BF16 = jnp.bfloat16

D_MODEL = 1024
DEPTH = 2
S5_GROUP = 16
S5_GROUPS = D_MODEL // S5_GROUP
S5_STATE = 64
SB_HEADS = 16
SB_HEAD_DIM = D_MODEL // SB_HEADS
D_FF = 4 * D_MODEL
DEEPNORM_ALPHA = (2.0 * DEPTH) ** 0.25
LN_EPS = 1e-5

LANES = 128
SUBLANES = 8
KB = 16
GROUPS_PER_TILE = LANES // S5_GROUP
N_TILES = D_MODEL // LANES
TILE_STATE = GROUPS_PER_TILE * S5_STATE
ALL_STATE = S5_GROUPS * S5_STATE
VMEM_LIMIT = 56 * 1024 * 1024
ATT_BLOCK = 128
ATT_PAIRS = 8
KEY_NORM_UNROLL = 4
FIRST_VISIT_BLOCKS = 3
EXP2_FLOOR = -130.0
LOG2_E = 1.0 / math.log(2.0)
NT_DIMS = (((1,), (1,)), ((), ()))


def _params(*sem):
    return pltpu.CompilerParams(dimension_semantics=sem, vmem_limit_bytes=VMEM_LIMIT)


def _resident(shape):
    zeros = (0,) * len(shape)
    return pl.BlockSpec(shape, lambda *_: zeros, pipeline_mode=pl.Buffered(1))


def _cmul(ar, ai, br, bi):
    return ar * br - ai * bi, ar * bi + ai * br


def _s5_prep_kernel(lr_ref, li_ref, ls_ref, br_ref, bi_ref, cr_ref, ci_ref,
                    car_ref, cai_ref, wr_ref, wi_ref, pwr_ref, pwi_ref,
                    a1r_ref, a1i_ref, pr_ref, pi_ref, bbr_ref, bbi_ref):
    tau = pl.program_id(0)

    @pl.when(tau == 0)
    def _():
        lr = lr_ref[...]
        li = li_ref[...]
        dt = jnp.exp(ls_ref[...])
        mag = jnp.exp(lr * dt)
        ang = li * dt
        a_re = mag * jnp.cos(ang)
        a_im = mag * jnp.sin(ang)
        nr = a_re - 1.0
        ni = a_im
        den = lr * lr + li * li
        f_re = (nr * lr + ni * li) / den
        f_im = (ni * lr - nr * li) / den
        bbr, bbi = _cmul(f_re, f_im, br_ref[...], bi_ref[...])
        bbr_ref[...] = bbr
        bbi_ref[...] = bbi
        a1r_ref[...] = a_re
        a1i_ref[...] = a_im
        pr_ref[...] = jnp.ones_like(a_re)
        pi_ref[...] = jnp.zeros_like(a_im)

    p_re = pr_ref[...]
    p_im = pi_ref[...]
    car, cai = _cmul(cr_ref[...], ci_ref[...], p_re, p_im)
    car_ref[...] = car
    cai_ref[...] = -cai
    wr, wi = _cmul(p_re, p_im, bbr_ref[...], bbi_ref[...])
    wr_ref[...] = wr
    wi_ref[...] = wi

    @pl.when(tau == KB)
    def _():
        q_re, q_im = p_re, p_im
        for i in range(SUBLANES):
            pwr_ref[i:i + 1, :] = q_re
            pwi_ref[i:i + 1, :] = q_im
            q_re, q_im = _cmul(q_re, q_im, p_re, p_im)

    n_re, n_im = _cmul(p_re, p_im, a1r_ref[...], a1i_ref[...])
    pr_ref[...] = n_re
    pi_ref[...] = n_im


def _s5_prep(lam_re, lam_im, log_step, b_re, b_im, c_re, c_im):
    g, p, h = S5_GROUPS, S5_STATE, S5_GROUP
    lr = lam_re.reshape(1, ALL_STATE)
    li = lam_im.reshape(1, ALL_STATE)
    ls = jnp.broadcast_to(log_step[:, None], (g, p)).reshape(1, ALL_STATE)
    br = b_re.transpose(2, 0, 1).reshape(h, ALL_STATE)
    bi = b_im.transpose(2, 0, 1).reshape(h, ALL_STATE)
    cr = c_re.transpose(1, 0, 2).reshape(h, ALL_STATE)
    ci = c_im.transpose(1, 0, 2).reshape(h, ALL_STATE)
    small = pl.BlockSpec((1, ALL_STATE), lambda t: (0, 0))
    big = pl.BlockSpec((h, ALL_STATE), lambda t: (0, 0))
    step = pl.BlockSpec((None, h, ALL_STATE), lambda t: (t, 0, 0))
    pw = pl.BlockSpec((SUBLANES, ALL_STATE), lambda t: (0, 0))
    out3 = jax.ShapeDtypeStruct((KB + 1, h, ALL_STATE), F32)
    outp = jax.ShapeDtypeStruct((SUBLANES, ALL_STATE), F32)
    return pl.pallas_call(
        _s5_prep_kernel,
        grid=(KB + 1,),
        in_specs=[small, small, small, big, big, big, big],
        out_specs=[step, step, step, step, pw, pw],
        out_shape=[out3, out3, out3, out3, outp, outp],
        scratch_shapes=[pltpu.VMEM((1, ALL_STATE), F32)] * 4 + [pltpu.VMEM((h, ALL_STATE), F32)] * 2,
        compiler_params=_params("arbitrary"),
        name="s5_prep",
    )(lr, li, ls, br, bi, cr, ci)


def _uproj_kernel(x_ref, w_ref, o_ref):
    o_ref[...] = jnp.dot(x_ref[...].astype(BF16), w_ref[...], preferred_element_type=F32)


def _u_proj(x, w_in):
    n = x.shape[0]
    tm = min(n, 1024)
    row = pl.BlockSpec((tm, D_MODEL), lambda i: (i, 0))
    return pl.pallas_call(
        _uproj_kernel,
        grid=(n // tm,),
        in_specs=[row, _resident((D_MODEL, D_MODEL))],
        out_specs=row,
        out_shape=jax.ShapeDtypeStruct((n, D_MODEL), F32),
        compiler_params=_params("parallel"),
        name="s5_u_proj",
    )(x, w_in)


def _shift_rows(x, k, row):
    return jnp.where(row >= k, pltpu.roll(x, k, 0), 0.0)


def _ssm_build_operators(car_ref, cai_ref, wr_ref, wi_ref, t_ref, win_ref, voutt_ref):
    ts = TILE_STATE
    row_group = lax.broadcasted_iota(jnp.int32, (LANES, ts), 0) // S5_GROUP
    lane_group = lax.broadcasted_iota(jnp.int32, (LANES, ts), 1) // S5_STATE
    same_group = row_group == lane_group

    def expand(re16, im16):
        tile = lambda a: jnp.where(same_group, jnp.concatenate([a] * GROUPS_PER_TILE, axis=0), 0.0)
        return jnp.concatenate([tile(re16), tile(im16)], axis=1)

    def split(a):
        hi = a.astype(BF16)
        return hi, (a - hi.astype(F32)).astype(BF16)

    def dot_nt(a, b):
        return lax.dot_general(a, b, NT_DIMS, preferred_element_type=F32)

    t_ref[...] = jnp.zeros_like(t_ref)
    bbar_hi, bbar_lo = split(expand(wr_ref[0], wi_ref[0]))
    for tau in range(KB + 1):
        ca_hi, ca_lo = split(expand(car_ref[tau], cai_ref[tau]))
        if tau >= 1:
            voutt_ref[(tau - 1) * LANES:tau * LANES, :] = ca_hi
        if tau < KB:
            lag = (dot_nt(bbar_hi, ca_hi) + dot_nt(bbar_hi, ca_lo) + dot_nt(bbar_lo, ca_hi)).astype(BF16)
            for j in range(KB - tau):
                t_ref[j * LANES:(j + 1) * LANES, (j + tau) * LANES:(j + tau + 1) * LANES] = lag
            win_ref[(KB - 1 - tau) * LANES:(KB - tau) * LANES, :] = expand(wr_ref[tau], wi_ref[tau]).astype(BF16)


def _ssm_kernel(u_ref, car_ref, cai_ref, wr_ref, wi_ref, pwr_ref, pwi_ref, d_ref, y_ref,
                s_ref, t_ref, win_ref, voutt_ref):
    nb = u_ref.shape[0] // KB
    ts = TILE_STATE

    @pl.when(pl.program_id(1) == 0)
    def _():
        _ssm_build_operators(car_ref, cai_ref, wr_ref, wi_ref, t_ref, win_ref, voutt_ref)

    u_cat = jnp.concatenate([u_ref[pl.ds(j, nb, stride=KB), :] for j in range(KB)], axis=1)
    u_bf = u_cat.astype(BF16)
    s_ref[...] = jnp.dot(u_bf, win_ref[...], preferred_element_type=F32)
    row = lax.broadcasted_iota(jnp.int32, (SUBLANES, ts), 0)
    pw_re = pwr_ref[...]
    pw_im = pwi_ref[...]

    def group(r, carry):
        c_re, c_im = carry
        rows = pl.ds(pl.multiple_of(r * SUBLANES, SUBLANES), SUBLANES)
        x_re = s_ref[rows, 0:ts]
        x_im = s_ref[rows, ts:2 * ts]
        for k in (1, 2, 4):
            a_re = pw_re[k - 1:k]
            a_im = pw_im[k - 1:k]
            sh_re = _shift_rows(x_re, k, row)
            sh_im = _shift_rows(x_im, k, row)
            x_re, x_im = x_re + a_re * sh_re - a_im * sh_im, x_im + a_re * sh_im + a_im * sh_re
        x_re, x_im = x_re + pw_re * c_re - pw_im * c_im, x_im + pw_re * c_im + pw_im * c_re
        s_ref[rows, 0:ts] = jnp.where(row >= 1, pltpu.roll(x_re, 1, 0), c_re)
        s_ref[rows, ts:2 * ts] = jnp.where(row >= 1, pltpu.roll(x_im, 1, 0), c_im)
        return x_re[SUBLANES - 1:SUBLANES], x_im[SUBLANES - 1:SUBLANES]

    zero = jnp.zeros((1, ts), F32)
    lax.fori_loop(0, nb // SUBLANES, group, (zero, zero))
    s_bf = s_ref[...].astype(BF16)
    width = 2 * LANES
    for c in range(KB * LANES // width):
        cols = slice(c * width, (c + 1) * width)
        live = (c + 1) * width
        y = jnp.dot(u_bf[:, :live], t_ref[0:live, cols], preferred_element_type=F32)
        y = y + lax.dot_general(s_bf, voutt_ref[cols, :], NT_DIMS, preferred_element_type=F32)
        y = y + d_ref[:, cols] * u_cat[:, cols]
        for jj in range(width // LANES):
            y_ref[pl.ds(c * (width // LANES) + jj, nb, stride=KB), :] = y[:, jj * LANES:(jj + 1) * LANES]


def _ssm(u, car, cai, wr, wi, pwr, pwi, d_tiled, batch):
    seq = u.shape[0] // batch
    nb = seq // KB
    factor = pl.BlockSpec((KB + 1, S5_GROUP, TILE_STATE), lambda c, b: (0, 0, c))
    power = pl.BlockSpec((SUBLANES, TILE_STATE), lambda c, b: (0, c))
    io = pl.BlockSpec((seq, LANES), lambda c, b: (b, c))
    return pl.pallas_call(
        _ssm_kernel,
        grid=(N_TILES, batch),
        in_specs=[io, factor, factor, factor, factor, power, power,
                  pl.BlockSpec((None, 1, KB * LANES), lambda c, b: (c, 0, 0))],
        out_specs=io,
        out_shape=jax.ShapeDtypeStruct(u.shape, F32),
        scratch_shapes=[pltpu.VMEM((nb, 2 * TILE_STATE), F32),
                        pltpu.VMEM((KB * LANES, KB * LANES), BF16),
                        pltpu.VMEM((KB * LANES, 2 * TILE_STATE), BF16),
                        pltpu.VMEM((KB * LANES, 2 * TILE_STATE), BF16)],
        compiler_params=_params("parallel", "arbitrary"),
        name="s5_ssm",
    )(u, car, cai, wr, wi, pwr, pwi, d_tiled)


def _layer_norm(z, g, b):
    mu = jnp.mean(z, axis=-1, keepdims=True)
    zc = z - mu
    var = jnp.mean(zc * zc, axis=-1, keepdims=True)
    return zc * lax.rsqrt(var + LN_EPS) * g + b


def _tail_kernel(*refs, glu):
    if glu:
        (m_ref, x_ref, wglu_ref, bglu_ref, wout_ref, g1_ref, b1n_ref,
         w1_ref, b1_ref, w2_ref, b2_ref, g2_ref, b2n_ref, o_ref) = refs
        g = jax.nn.gelu(m_ref[...])
        gate = jnp.dot(g.astype(BF16), wglu_ref[...], preferred_element_type=F32) + bglu_ref[...]
        m = (g * jax.nn.sigmoid(gate)).astype(BF16)
    else:
        (m_ref, x_ref, wout_ref, g1_ref, b1n_ref,
         w1_ref, b1_ref, w2_ref, b2_ref, g2_ref, b2n_ref, o_ref) = refs
        m = m_ref[...]
    mix = jnp.dot(m, wout_ref[...], preferred_element_type=F32)
    x1 = _layer_norm(DEEPNORM_ALPHA * x_ref[...] + mix, g1_ref[...], b1n_ref[...])
    x1_bf = x1.astype(BF16)
    ff = jnp.zeros_like(x1)
    for c in range(D_FF // D_MODEL):
        cols = slice(c * D_MODEL, (c + 1) * D_MODEL)
        hid = jnp.dot(x1_bf, w1_ref[:, cols], preferred_element_type=F32) + b1_ref[:, cols]
        hid = jnp.square(jnp.maximum(hid, 0.0))
        ff = ff + jnp.dot(hid.astype(BF16), w2_ref[cols, :], preferred_element_type=F32)
    ff = ff + b2_ref[...]
    o_ref[...] = _layer_norm(DEEPNORM_ALPHA * x1 + ff, g2_ref[...], b2n_ref[...])


def _layer_tail(m, x, weights, glu):
    n = x.shape[0]
    tm = min(n, 512)
    row = pl.BlockSpec((tm, D_MODEL), lambda i: (i, 0))
    ops = [m, x]
    specs = [row, row]
    for wgt in weights:
        a = wgt.reshape(1, -1) if wgt.ndim == 1 else wgt
        ops.append(a)
        specs.append(_resident(a.shape))
    return pl.pallas_call(
        functools.partial(_tail_kernel, glu=glu),
        grid=(n // tm,),
        in_specs=specs,
        out_specs=row,
        out_shape=jax.ShapeDtypeStruct((n, D_MODEL), F32),
        compiler_params=_params("parallel"),
        name="layer_tail_glu" if glu else "layer_tail",
    )(*ops)


def _qkv_kernel(x_ref, w_ref, q_ref, k_ref, v_ref):
    y = jnp.dot(x_ref[...].astype(BF16), w_ref[...], preferred_element_type=F32)
    q_ref[...] = (y[:, :D_MODEL] * (LOG2_E / math.sqrt(SB_HEAD_DIM))).astype(BF16)
    k_ref[...] = y[:, D_MODEL:2 * D_MODEL].astype(BF16)
    v_ref[...] = y[:, 2 * D_MODEL:].astype(BF16)


def _qkv(x, w_qkv):
    n = x.shape[0]
    tm = min(n, 512)
    row = pl.BlockSpec((tm, D_MODEL), lambda i: (i, 0))
    out = jax.ShapeDtypeStruct((n, D_MODEL), BF16)
    return pl.pallas_call(
        _qkv_kernel,
        grid=(n // tm,),
        in_specs=[row, _resident(w_qkv.shape)],
        out_specs=[row, row, row],
        out_shape=[out, out, out],
        compiler_params=_params("parallel"),
        name="qkv_proj",
    )(x, w_qkv)


def _attn_kernel(q_ref, k_ref, v_ref, ntri_ref, hsum_ref, o_ref, acc_ref, r_ref, st_ref):
    i = pl.program_id(2)
    tb = ATT_BLOCK
    pairs = q_ref.shape[1] // LANES
    lane = lax.broadcasted_iota(jnp.int32, (tb, LANES), 1)
    first = lane < SB_HEAD_DIM

    def split_heads(a):
        zero = jnp.zeros_like(a)
        return jnp.concatenate([jnp.where(first, a, zero), jnp.where(first, zero, a)], axis=0)

    def head_sq_norms(a):
        sq = jnp.square(a.astype(F32)).astype(BF16)
        return jnp.dot(sq, hsum_ref[...], preferred_element_type=F32)

    def pair_max_sq_norm(ref, rows):
        m = head_sq_norms(ref[rows, 0:LANES])
        for p in range(1, pairs):
            m = jnp.maximum(m, head_sq_norms(ref[rows, p * LANES:(p + 1) * LANES]))
        return m

    @pl.when(i == 0)
    def _():
        def part(c, m):
            for sub in range(KEY_NORM_UNROLL):
                rows = pl.ds(pl.multiple_of((c * KEY_NORM_UNROLL + sub) * tb, tb), tb)
                m = jnp.maximum(m, pair_max_sq_norm(k_ref, rows))
            return m
        trips = k_ref.shape[0] // (tb * KEY_NORM_UNROLL)
        st_ref[0] = jnp.max(lax.fori_loop(0, trips, part, jnp.zeros((tb, LANES), F32)))

    acc_ref[...] = jnp.zeros_like(acc_ref)
    r_ref[...] = jnp.zeros_like(r_ref)
    rr = lax.broadcasted_iota(jnp.int32, (tb, 2 * tb), 0)
    cc = lax.broadcasted_iota(jnp.int32, (tb, 2 * tb), 1)
    causal = jnp.bitwise_and(cc, tb - 1) < rr
    lanes = [slice(p * LANES, (p + 1) * LANES) for p in range(pairs)]

    def visit(blocks, diagonal):
        keys = [pl.ds(pl.multiple_of(j * tb, tb), tb) for j in blocks]
        todo = [(p, t) for p in range(pairs) for t in range(len(blocks))]
        z = {pt: lax.dot_general(q_ref[:, lanes[pt[0]]], split_heads(k_ref[keys[pt[1]], lanes[pt[0]]]), NT_DIMS,
                                 preferred_element_type=F32) for pt in todo}
        cost = {}
        for pt in todo:
            c = jnp.maximum(z[pt], 0.0) + jnp.log2(1.0 + 1.0 / jnp.exp2(jnp.abs(z[pt])))
            cost[pt] = jnp.where(causal, c, 0.0) if (diagonal and pt[1] == 0) else c
        suffix = {}
        for pt in todo:
            hi = cost[pt].astype(BF16)
            lo = (cost[pt] - hi.astype(F32)).astype(BF16)
            suffix[pt] = jnp.dot(jnp.concatenate([hi, lo], axis=1), ntri_ref[...], preferred_element_type=F32)
        r_top = None
        for p in range(pairs):
            r = r_ref[p]
            weights = []
            for t in range(len(blocks)):
                w = jnp.exp2(z[p, t] + suffix[p, t] + r)
                if diagonal and t == 0:
                    w = jnp.where(causal, w, 0.0)
                weights.append(w.astype(BF16))
                spent = [jnp.broadcast_to(jnp.sum(cost[p, t][:, h * tb:(h + 1) * tb], axis=1, keepdims=True),
                                          (tb, tb)) for h in range(2)]
                r = r - jnp.concatenate(spent, axis=1)
            values = jnp.concatenate([split_heads(v_ref[keys[t], lanes[p]]) for t in range(len(blocks))], axis=0)
            acc_ref[:, lanes[p]] += jnp.dot(jnp.concatenate(weights, axis=1), values, preferred_element_type=F32)
            r_ref[p] = r
            r_top = r if r_top is None else jnp.maximum(r_top, r)
        return jnp.max(r_top)

    head_start = FIRST_VISIT_BLOCKS - 1

    @pl.when(i >= head_start)
    def _():
        st_ref[1] = visit([i - t for t in range(FIRST_VISIT_BLOCKS)], True)

    @pl.when(i < head_start)
    def _():
        st_ref[1] = visit([i], True)

    qn = jnp.max(pair_max_sq_norm(q_ref, slice(None)))
    z_bound = jnp.sqrt(qn * st_ref[0]) * 1.02 + 1e-3

    def more(c):
        j, r_max = c
        return jnp.logical_and(j >= 0, r_max + z_bound > EXP2_FLOOR)

    def step(c):
        j, _ = c
        return j - 1, visit([j], False)

    lax.while_loop(more, step, (jnp.where(i >= head_start, i - FIRST_VISIT_BLOCKS, i - 1), st_ref[1]))
    o_ref[...] = acc_ref[...].astype(o_ref.dtype)


def _attention(q, k, v, batch):
    n = q.shape[0]
    seq = n // batch
    tb = ATT_BLOCK
    nq = seq // tb
    width = ATT_PAIRS * LANES
    idx = jnp.arange(2 * tb)
    same_head = (idx[:, None] // tb) == (idx[None, :] // tb)
    ntri = -(same_head & (idx[:, None] >= idx[None, :])).astype(BF16)
    ntri = jnp.concatenate([ntri, ntri], axis=0)
    hd = jnp.arange(LANES) // SB_HEAD_DIM
    hsum = (hd[:, None] == hd[None, :]).astype(BF16)
    kv_spec = pl.BlockSpec((seq, width), lambda b, h, i: (b, h))
    q_spec = pl.BlockSpec((tb, width), lambda b, h, i: (b * nq + i, h))
    return pl.pallas_call(
        _attn_kernel,
        grid=(batch, D_MODEL // width, nq),
        in_specs=[q_spec, kv_spec, kv_spec, _resident(ntri.shape), _resident(hsum.shape)],
        out_specs=q_spec,
        out_shape=jax.ShapeDtypeStruct((n, D_MODEL), BF16),
        scratch_shapes=[pltpu.VMEM((tb, width), F32), pltpu.VMEM((ATT_PAIRS, tb, 2 * tb), F32),
                        pltpu.SMEM((2,), F32)],
        compiler_params=_params("parallel", "parallel", "arbitrary"),
        name="sb_attention",
    )(q, k, v, ntri, hsum)


def kernel(x, s5_w_in, s5_lambda_re, s5_lambda_im, s5_b_re, s5_b_im, s5_c_re, s5_c_im, s5_d, s5_log_step,
           s5_w_glu, s5_b_glu, s5_w_out, sb_w_kv, sb_w_q, sb_w_out, mlp_w1, mlp_b1, mlp_w2, mlp_b2,
           ln_mix_g, ln_mix_b, ln_mlp_g, ln_mlp_b):
    batch, seq, d = x.shape
    n = batch * seq
    bf = lambda a: a.astype(BF16)
    x0 = x.reshape(n, d)

    car, cai, wr, wi, pwr, pwi = _s5_prep(s5_lambda_re[0], s5_lambda_im[0], s5_log_step[0],
                                          s5_b_re[0], s5_b_im[0], s5_c_re[0], s5_c_im[0])
    d_tiled = jnp.tile(s5_d[0].reshape(N_TILES, 1, LANES), (1, 1, KB))
    u = _u_proj(x0, bf(s5_w_in[0]))
    y = _ssm(u, car, cai, wr, wi, pwr, pwi, d_tiled, batch)
    x1 = _layer_tail(
        y, x0,
        [bf(s5_w_glu[0]), s5_b_glu[0], bf(s5_w_out[0]), ln_mix_g[0], ln_mix_b[0],
         bf(mlp_w1[0]), mlp_b1[0], bf(mlp_w2[0]), mlp_b2[0], ln_mlp_g[0], ln_mlp_b[0]],
        glu=True)

    q, k, v = _qkv(x1, bf(jnp.concatenate([sb_w_q[0], sb_w_kv], axis=1)))
    o = _attention(q, k, v, batch)
    out = _layer_tail(
        o, x1,
        [bf(sb_w_out[0]), ln_mix_g[1], ln_mix_b[1],
         bf(mlp_w1[1]), mlp_b1[1], bf(mlp_w2[1]), mlp_b2[1], ln_mlp_g[1], ln_mlp_b[1]],
        glu=False)
    return out.reshape(batch, seq, d)
```

```python
import functools
import math

import jax
import jax.numpy as jnp
from jax import lax
from jax.experimental import pallas as pl
from jax.experimental.pallas import tpu as pltpu

F32 = jnp.float32
BF16 = jnp.bfloat16

D_MODEL = 1024
DEPTH = 2
S5_GROUP = 16
S5_GROUPS = D_MODEL // S5_GROUP
S5_STATE = 64
SB_HEADS = 16
SB_HEAD_DIM = D_MODEL // SB_HEADS
D_FF = 4 * D_MODEL
DEEPNORM_ALPHA = (2.0 * DEPTH) ** 0.25
LN_EPS = 1e-5

LANES = 128
SUBLANES = 8
KB = 16
GROUPS_PER_TILE = LANES // S5_GROUP
N_TILES = D_MODEL // LANES
TILE_STATE = GROUPS_PER_TILE * S5_STATE
ALL_STATE = S5_GROUPS * S5_STATE
VMEM_LIMIT = 56 * 1024 * 1024
TAIL_SPLIT = 2
ATT_BLOCK = 128
ATT_PAIRS = 8
KEY_NORM_UNROLL = 4
FIRST_VISIT_BLOCKS = 3
EXP2_FLOOR = -130.0
BOUNDED_SCORE = 120.0
LOG2_E = 1.0 / math.log(2.0)
NT_DIMS = (((1,), (1,)), ((), ()))


def _params(*sem):
    return pltpu.CompilerParams(dimension_semantics=sem, vmem_limit_bytes=VMEM_LIMIT)


def _resident(shape):
    zeros = (0,) * len(shape)
    return pl.BlockSpec(shape, lambda *_: zeros, pipeline_mode=pl.Buffered(1))


def _cmul(ar, ai, br, bi):
    return ar * br - ai * bi, ar * bi + ai * br


def _s5_prep_kernel(lr_ref, li_ref, ls_ref, br_ref, bi_ref, cr_ref, ci_ref,
                    car_ref, cai_ref, wr_ref, wi_ref, pwr_ref, pwi_ref,
                    a1r_ref, a1i_ref, pr_ref, pi_ref, bbr_ref, bbi_ref):
    tau = pl.program_id(0)

    @pl.when(tau == 0)
    def _():
        lr = lr_ref[...]
        li = li_ref[...]
        dt = jnp.exp(ls_ref[...])
        mag = jnp.exp(lr * dt)
        ang = li * dt
        a_re = mag * jnp.cos(ang)
        a_im = mag * jnp.sin(ang)
        nr = a_re - 1.0
        ni = a_im
        den = lr * lr + li * li
        f_re = (nr * lr + ni * li) / den
        f_im = (ni * lr - nr * li) / den
        bbr, bbi = _cmul(f_re, f_im, br_ref[...], bi_ref[...])
        bbr_ref[...] = bbr
        bbi_ref[...] = bbi
        a1r_ref[...] = a_re
        a1i_ref[...] = a_im
        pr_ref[...] = jnp.ones_like(a_re)
        pi_ref[...] = jnp.zeros_like(a_im)

    p_re = pr_ref[...]
    p_im = pi_ref[...]
    car, cai = _cmul(cr_ref[...], ci_ref[...], p_re, p_im)
    car_ref[...] = car
    cai_ref[...] = -cai
    wr, wi = _cmul(p_re, p_im, bbr_ref[...], bbi_ref[...])
    wr_ref[...] = wr
    wi_ref[...] = wi

    @pl.when(tau == KB)
    def _():
        q_re, q_im = p_re, p_im
        for i in range(SUBLANES):
            pwr_ref[i:i + 1, :] = q_re
            pwi_ref[i:i + 1, :] = q_im
            q_re, q_im = _cmul(q_re, q_im, p_re, p_im)

    n_re, n_im = _cmul(p_re, p_im, a1r_ref[...], a1i_ref[...])
    pr_ref[...] = n_re
    pi_ref[...] = n_im


def _s5_prep(lam_re, lam_im, log_step, b_re, b_im, c_re, c_im):
    g, p, h = S5_GROUPS, S5_STATE, S5_GROUP
    lr = lam_re.reshape(1, ALL_STATE)
    li = lam_im.reshape(1, ALL_STATE)
    ls = jnp.broadcast_to(log_step[:, None], (g, p)).reshape(1, ALL_STATE)
    br = b_re.transpose(2, 0, 1).reshape(h, ALL_STATE)
    bi = b_im.transpose(2, 0, 1).reshape(h, ALL_STATE)
    cr = c_re.transpose(1, 0, 2).reshape(h, ALL_STATE)
    ci = c_im.transpose(1, 0, 2).reshape(h, ALL_STATE)
    small = pl.BlockSpec((1, ALL_STATE), lambda t: (0, 0))
    big = pl.BlockSpec((h, ALL_STATE), lambda t: (0, 0))
    step = pl.BlockSpec((None, h, ALL_STATE), lambda t: (t, 0, 0))
    pw = pl.BlockSpec((SUBLANES, ALL_STATE), lambda t: (0, 0))
    out3 = jax.ShapeDtypeStruct((KB + 1, h, ALL_STATE), F32)
    outp = jax.ShapeDtypeStruct((SUBLANES, ALL_STATE), F32)
    return pl.pallas_call(
        _s5_prep_kernel,
        grid=(KB + 1,),
        in_specs=[small, small, small, big, big, big, big],
        out_specs=[step, step, step, step, pw, pw],
        out_shape=[out3, out3, out3, out3, outp, outp],
        scratch_shapes=[pltpu.VMEM((1, ALL_STATE), F32)] * 4 + [pltpu.VMEM((h, ALL_STATE), F32)] * 2,
        compiler_params=_params("arbitrary"),
        name="s5_prep",
    )(lr, li, ls, br, bi, cr, ci)


def _uproj_kernel(x_ref, w_ref, o_ref):
    o_ref[...] = jnp.dot(x_ref[...].astype(BF16), w_ref[...], preferred_element_type=F32)


def _u_proj(x, w_in):
    n = x.shape[0]
    tm = min(n, 1024)
    row = pl.BlockSpec((tm, D_MODEL), lambda i: (i, 0))
    return pl.pallas_call(
        _uproj_kernel,
        grid=(n // tm,),
        in_specs=[row, _resident((D_MODEL, D_MODEL))],
        out_specs=row,
        out_shape=jax.ShapeDtypeStruct((n, D_MODEL), F32),
        compiler_params=_params("parallel"),
        name="s5_u_proj",
    )(x, w_in)


def _shift_rows(x, k, row):
    return jnp.where(row >= k, pltpu.roll(x, k, 0), 0.0)


def _ssm_build_operators(car_ref, cai_ref, wr_ref, wi_ref, t_ref, win_ref, voutt_ref):
    ts = TILE_STATE
    row_group = lax.broadcasted_iota(jnp.int32, (LANES, ts), 0) // S5_GROUP
    lane_group = lax.broadcasted_iota(jnp.int32, (LANES, ts), 1) // S5_STATE
    same_group = row_group == lane_group

    def expand(re16, im16):
        tile = lambda a: jnp.where(same_group, jnp.concatenate([a] * GROUPS_PER_TILE, axis=0), 0.0)
        return jnp.concatenate([tile(re16), tile(im16)], axis=1)

    def split(a):
        hi = a.astype(BF16)
        return hi, (a - hi.astype(F32)).astype(BF16)

    def dot_nt(a, b):
        return lax.dot_general(a, b, NT_DIMS, preferred_element_type=F32)

    t_ref[...] = jnp.zeros_like(t_ref)
    bbar_hi, bbar_lo = split(expand(wr_ref[0], wi_ref[0]))
    for tau in range(KB + 1):
        ca_hi, ca_lo = split(expand(car_ref[tau], cai_ref[tau]))
        if tau >= 1:
            voutt_ref[(tau - 1) * LANES:tau * LANES, :] = ca_hi
        if tau < KB:
            lag = (dot_nt(bbar_hi, ca_hi) + dot_nt(bbar_hi, ca_lo) + dot_nt(bbar_lo, ca_hi)).astype(BF16)
            for j in range(KB - tau):
                t_ref[j * LANES:(j + 1) * LANES, (j + tau) * LANES:(j + tau + 1) * LANES] = lag
            win_ref[(KB - 1 - tau) * LANES:(KB - tau) * LANES, :] = expand(wr_ref[tau], wi_ref[tau]).astype(BF16)


def _ssm_kernel(u_ref, car_ref, cai_ref, wr_ref, wi_ref, pwr_ref, pwi_ref, d_ref, y_ref,
                s_ref, t_ref, win_ref, voutt_ref):
    nb = u_ref.shape[0] // KB
    ts = TILE_STATE

    @pl.when(pl.program_id(1) == 0)
    def _():
        _ssm_build_operators(car_ref, cai_ref, wr_ref, wi_ref, t_ref, win_ref, voutt_ref)

    u_cat = jnp.concatenate([u_ref[pl.ds(j, nb, stride=KB), :] for j in range(KB)], axis=1)
    u_bf = u_cat.astype(BF16)
    s_ref[...] = jnp.dot(u_bf, win_ref[...], preferred_element_type=F32)
    row = lax.broadcasted_iota(jnp.int32, (SUBLANES, ts), 0)
    pw_re = pwr_ref[...]
    pw_im = pwi_ref[...]

    def group(r, carry):
        c_re, c_im = carry
        rows = pl.ds(pl.multiple_of(r * SUBLANES, SUBLANES), SUBLANES)
        x_re = s_ref[rows, 0:ts]
        x_im = s_ref[rows, ts:2 * ts]
        for k in (1, 2, 4):
            a_re = pw_re[k - 1:k]
            a_im = pw_im[k - 1:k]
            sh_re = _shift_rows(x_re, k, row)
            sh_im = _shift_rows(x_im, k, row)
            x_re, x_im = x_re + a_re * sh_re - a_im * sh_im, x_im + a_re * sh_im + a_im * sh_re
        x_re, x_im = x_re + pw_re * c_re - pw_im * c_im, x_im + pw_re * c_im + pw_im * c_re
        s_ref[rows, 0:ts] = jnp.where(row >= 1, pltpu.roll(x_re, 1, 0), c_re)
        s_ref[rows, ts:2 * ts] = jnp.where(row >= 1, pltpu.roll(x_im, 1, 0), c_im)
        return x_re[SUBLANES - 1:SUBLANES], x_im[SUBLANES - 1:SUBLANES]

    zero = jnp.zeros((1, ts), F32)
    lax.fori_loop(0, nb // SUBLANES, group, (zero, zero))
    s_bf = s_ref[...].astype(BF16)
    width = 2 * LANES
    for c in range(KB * LANES // width):
        cols = slice(c * width, (c + 1) * width)
        live = (c + 1) * width
        y = jnp.dot(u_bf[:, :live], t_ref[0:live, cols], preferred_element_type=F32)
        y = y + lax.dot_general(s_bf, voutt_ref[cols, :], NT_DIMS, preferred_element_type=F32)
        y = y + d_ref[:, cols] * u_cat[:, cols]
        for jj in range(width // LANES):
            y_ref[pl.ds(c * (width // LANES) + jj, nb, stride=KB), :] = y[:, jj * LANES:(jj + 1) * LANES]


def _ssm(u, car, cai, wr, wi, pwr, pwi, d_tiled, batch):
    seq = u.shape[0] // batch
    nb = seq // KB
    factor = pl.BlockSpec((KB + 1, S5_GROUP, TILE_STATE), lambda c, b: (0, 0, c))
    power = pl.BlockSpec((SUBLANES, TILE_STATE), lambda c, b: (0, c))
    io = pl.BlockSpec((seq, LANES), lambda c, b: (b, c))
    return pl.pallas_call(
        _ssm_kernel,
        grid=(N_TILES, batch),
        in_specs=[io, factor, factor, factor, factor, power, power,
                  pl.BlockSpec((None, 1, KB * LANES), lambda c, b: (c, 0, 0))],
        out_specs=io,
        out_shape=jax.ShapeDtypeStruct(u.shape, F32),
        scratch_shapes=[pltpu.VMEM((nb, 2 * TILE_STATE), F32),
                        pltpu.VMEM((KB * LANES, KB * LANES), BF16),
                        pltpu.VMEM((KB * LANES, 2 * TILE_STATE), BF16),
                        pltpu.VMEM((KB * LANES, 2 * TILE_STATE), BF16)],
        compiler_params=_params("parallel", "arbitrary"),
        name="s5_ssm",
    )(u, car, cai, wr, wi, pwr, pwi, d_tiled)


def _layer_norm(z, g, b):
    mu = jnp.mean(z, axis=-1, keepdims=True)
    zc = z - mu
    var = jnp.mean(zc * zc, axis=-1, keepdims=True)
    return zc * lax.rsqrt(var + LN_EPS) * g + b


def _tail_kernel(*refs, glu):
    if glu:
        (m_ref, x_ref, wglu_ref, bglu_ref, wout_ref, g1_ref, b1n_ref,
         w1_ref, b1_ref, w2_ref, b2_ref, g2_ref, b2n_ref, o_ref) = refs
    else:
        (m_ref, x_ref, wout_ref, g1_ref, b1n_ref,
         w1_ref, b1_ref, w2_ref, b2_ref, g2_ref, b2n_ref, o_ref) = refs
    tm = x_ref.shape[0]
    parts = [slice(h * tm // TAIL_SPLIT, (h + 1) * tm // TAIL_SPLIT) for h in range(TAIL_SPLIT)]
    mm = lambda a, b: jnp.dot(a, b, preferred_element_type=F32)
    if glu:
        g = [jax.nn.gelu(m_ref[rows, :]) for rows in parts]
        gate = [mm(gh.astype(BF16), wglu_ref[...]) + bglu_ref[...] for gh in g]
        m = [(gh * jax.nn.sigmoid(th)).astype(BF16) for gh, th in zip(g, gate)]
    else:
        m = [m_ref[rows, :] for rows in parts]
    mix = [mm(mh, wout_ref[...]) for mh in m]
    x1 = [_layer_norm(DEEPNORM_ALPHA * x_ref[rows, :] + mh, g1_ref[...], b1n_ref[...])
          for rows, mh in zip(parts, mix)]
    x1_bf = [a.astype(BF16) for a in x1]
    ff = [None] * TAIL_SPLIT
    for c in range(D_FF // D_MODEL):
        cols = slice(c * D_MODEL, (c + 1) * D_MODEL)
        hid = [jnp.square(jnp.maximum(mm(a, w1_ref[:, cols]) + b1_ref[:, cols], 0.0)) for a in x1_bf]
        for h in range(TAIL_SPLIT):
            out = mm(hid[h].astype(BF16), w2_ref[cols, :])
            ff[h] = out if c == 0 else ff[h] + out
    for h, rows in enumerate(parts):
        o_ref[rows, :] = _layer_norm(DEEPNORM_ALPHA * x1[h] + ff[h] + b2_ref[...], g2_ref[...], b2n_ref[...])


def _layer_tail(m, x, weights, glu):
    n = x.shape[0]
    tm = min(n, 512)
    row = pl.BlockSpec((tm, D_MODEL), lambda i: (i, 0))
    ops = [m, x]
    specs = [row, row]
    for wgt in weights:
        a = wgt.reshape(1, -1) if wgt.ndim == 1 else wgt
        ops.append(a)
        specs.append(_resident(a.shape))
    return pl.pallas_call(
        functools.partial(_tail_kernel, glu=glu),
        grid=(n // tm,),
        in_specs=specs,
        out_specs=row,
        out_shape=jax.ShapeDtypeStruct((n, D_MODEL), F32),
        compiler_params=_params("parallel"),
        name="layer_tail_glu" if glu else "layer_tail",
    )(*ops)


def _qkv_kernel(x_ref, w_ref, q_ref, k_ref, v_ref):
    y = jnp.dot(x_ref[...].astype(BF16), w_ref[...], preferred_element_type=F32)
    q_ref[...] = (y[:, :D_MODEL] * (LOG2_E / math.sqrt(SB_HEAD_DIM))).astype(BF16)
    k_ref[...] = y[:, D_MODEL:2 * D_MODEL].astype(BF16)
    v_ref[...] = y[:, 2 * D_MODEL:].astype(BF16)


def _qkv(x, w_qkv):
    n = x.shape[0]
    tm = min(n, 512)
    row = pl.BlockSpec((tm, D_MODEL), lambda i: (i, 0))
    out = jax.ShapeDtypeStruct((n, D_MODEL), BF16)
    return pl.pallas_call(
        _qkv_kernel,
        grid=(n // tm,),
        in_specs=[row, _resident(w_qkv.shape)],
        out_specs=[row, row, row],
        out_shape=[out, out, out],
        compiler_params=_params("parallel"),
        name="qkv_proj",
    )(x, w_qkv)


def _attn_kernel(q_ref, k_ref, v_ref, ntri_ref, hsum_ref, o_ref, acc_ref, r_ref, st_ref):
    i = pl.program_id(2)
    tb = ATT_BLOCK
    pairs = q_ref.shape[1] // LANES
    lane = lax.broadcasted_iota(jnp.int32, (tb, LANES), 1)
    first = lane < SB_HEAD_DIM

    def split_heads(a):
        zero = jnp.zeros_like(a)
        return jnp.concatenate([jnp.where(first, a, zero), jnp.where(first, zero, a)], axis=0)

    def head_sq_norms(a):
        sq = jnp.square(a.astype(F32)).astype(BF16)
        return jnp.dot(sq, hsum_ref[...], preferred_element_type=F32)

    def pair_max_sq_norm(ref, rows):
        m = head_sq_norms(ref[rows, 0:LANES])
        for p in range(1, pairs):
            m = jnp.maximum(m, head_sq_norms(ref[rows, p * LANES:(p + 1) * LANES]))
        return m

    @pl.when(i == 0)
    def _():
        def part(c, m):
            for sub in range(KEY_NORM_UNROLL):
                rows = pl.ds(pl.multiple_of((c * KEY_NORM_UNROLL + sub) * tb, tb), tb)
                m = jnp.maximum(m, pair_max_sq_norm(k_ref, rows))
            return m
        trips = k_ref.shape[0] // (tb * KEY_NORM_UNROLL)
        st_ref[0] = jnp.max(lax.fori_loop(0, trips, part, jnp.zeros((tb, LANES), F32)))

    acc_ref[...] = jnp.zeros_like(acc_ref)
    r_ref[...] = jnp.zeros_like(r_ref)
    rr = lax.broadcasted_iota(jnp.int32, (tb, 2 * tb), 0)
    cc = lax.broadcasted_iota(jnp.int32, (tb, 2 * tb), 1)
    causal = jnp.bitwise_and(cc, tb - 1) < rr
    lanes = [slice(p * LANES, (p + 1) * LANES) for p in range(pairs)]

    def visit(blocks, first):
        diagonal = first
        keys = [pl.ds(pl.multiple_of(j * tb, tb), tb) for j in blocks]
        todo = [(p, t) for p in range(pairs) for t in range(len(blocks))]
        z = {pt: lax.dot_general(q_ref[:, lanes[pt[0]]], split_heads(k_ref[keys[pt[1]], lanes[pt[0]]]), NT_DIMS,
                                 preferred_element_type=F32) for pt in todo}

        def finish(no_overflow):
            cost = {}
            for pt in todo:
                if no_overflow:
                    c = jnp.log2(1.0 + jnp.exp2(z[pt]))
                else:
                    c = jnp.maximum(z[pt], 0.0) + jnp.log2(1.0 + 1.0 / jnp.exp2(jnp.abs(z[pt])))
                cost[pt] = jnp.where(causal, c, 0.0) if (diagonal and pt[1] == 0) else c
            suffix = {}
            for pt in todo:
                hi = cost[pt].astype(BF16)
                lo = (cost[pt] - hi.astype(F32)).astype(BF16)
                suffix[pt] = jnp.dot(jnp.concatenate([hi, lo], axis=1), ntri_ref[...], preferred_element_type=F32)
            r_top = None
            for p in range(pairs):
                r = r_ref[p]
                weights = []
                for t in range(len(blocks)):
                    w = jnp.exp2(z[p, t] + suffix[p, t] + r)
                    if diagonal and t == 0:
                        w = jnp.where(causal, w, 0.0)
                    weights.append(w.astype(BF16))
                    spent = [jnp.broadcast_to(jnp.sum(cost[p, t][:, h * tb:(h + 1) * tb], axis=1, keepdims=True),
                                              (tb, tb)) for h in range(2)]
                    r = r - jnp.concatenate(spent, axis=1)
                values = jnp.concatenate([split_heads(v_ref[keys[t], lanes[p]]) for t in range(len(blocks))], axis=0)
                acc_ref[:, lanes[p]] += jnp.dot(jnp.concatenate(weights, axis=1), values,
                                                preferred_element_type=F32)
                r_ref[p] = r
                r_top = r if r_top is None else jnp.maximum(r_top, r)
            st_ref[1] = jnp.max(r_top)

        if first:
            qn = jnp.max(pair_max_sq_norm(q_ref, slice(None)))
            st_ref[2] = jnp.sqrt(qn * st_ref[0]) * 1.02 + 1e-3
        if first and len(blocks) > 1:
            bounded = st_ref[2] < BOUNDED_SCORE
            pl.when(bounded)(functools.partial(finish, True))
            pl.when(jnp.logical_not(bounded))(functools.partial(finish, False))
        else:
            finish(False)
        return st_ref[1]

    head_start = FIRST_VISIT_BLOCKS - 1

    @pl.when(i >= head_start)
    def _():
        visit([i - t for t in range(FIRST_VISIT_BLOCKS)], True)

    @pl.when(i < head_start)
    def _():
        visit([i], True)

    def more(c):
        j, r_max = c
        return jnp.logical_and(j >= 0, r_max + st_ref[2] > EXP2_FLOOR)

    def step(c):
        j, _ = c
        return j - 1, visit([j], False)

    lax.while_loop(more, step, (jnp.where(i >= head_start, i - FIRST_VISIT_BLOCKS, i - 1), st_ref[1]))
    o_ref[...] = acc_ref[...].astype(o_ref.dtype)


def _attention(q, k, v, batch):
    n = q.shape[0]
    seq = n // batch
    tb = ATT_BLOCK
    nq = seq // tb
    width = ATT_PAIRS * LANES
    idx = jnp.arange(2 * tb)
    same_head = (idx[:, None] // tb) == (idx[None, :] // tb)
    ntri = -(same_head & (idx[:, None] >= idx[None, :])).astype(BF16)
    ntri = jnp.concatenate([ntri, ntri], axis=0)
    hd = jnp.arange(LANES) // SB_HEAD_DIM
    hsum = (hd[:, None] == hd[None, :]).astype(BF16)
    kv_spec = pl.BlockSpec((seq, width), lambda b, h, i: (b, h))
    q_spec = pl.BlockSpec((tb, width), lambda b, h, i: (b * nq + i, h))
    return pl.pallas_call(
        _attn_kernel,
        grid=(batch, D_MODEL // width, nq),
        in_specs=[q_spec, kv_spec, kv_spec, _resident(ntri.shape), _resident(hsum.shape)],
        out_specs=q_spec,
        out_shape=jax.ShapeDtypeStruct((n, D_MODEL), BF16),
        scratch_shapes=[pltpu.VMEM((tb, width), F32), pltpu.VMEM((ATT_PAIRS, tb, 2 * tb), F32),
                        pltpu.SMEM((3,), F32)],
        compiler_params=_params("parallel", "parallel", "arbitrary"),
        name="sb_attention",
    )(q, k, v, ntri, hsum)


def kernel(x, s5_w_in, s5_lambda_re, s5_lambda_im, s5_b_re, s5_b_im, s5_c_re, s5_c_im, s5_d, s5_log_step,
           s5_w_glu, s5_b_glu, s5_w_out, sb_w_kv, sb_w_q, sb_w_out, mlp_w1, mlp_b1, mlp_w2, mlp_b2,
           ln_mix_g, ln_mix_b, ln_mlp_g, ln_mlp_b):
    batch, seq, d = x.shape
    n = batch * seq
    bf = lambda a: a.astype(BF16)
    x0 = x.reshape(n, d)

    car, cai, wr, wi, pwr, pwi = _s5_prep(s5_lambda_re[0], s5_lambda_im[0], s5_log_step[0],
                                          s5_b_re[0], s5_b_im[0], s5_c_re[0], s5_c_im[0])
    d_tiled = jnp.tile(s5_d[0].reshape(N_TILES, 1, LANES), (1, 1, KB))
    u = _u_proj(x0, bf(s5_w_in[0]))
    y = _ssm(u, car, cai, wr, wi, pwr, pwi, d_tiled, batch)
    x1 = _layer_tail(
        y, x0,
        [bf(s5_w_glu[0]), s5_b_glu[0], bf(s5_w_out[0]), ln_mix_g[0], ln_mix_b[0],
         bf(mlp_w1[0]), mlp_b1[0], bf(mlp_w2[0]), mlp_b2[0], ln_mlp_g[0], ln_mlp_b[0]],
        glu=True)

    q, k, v = _qkv(x1, bf(jnp.concatenate([sb_w_q[0], sb_w_kv], axis=1)))
    o = _attention(q, k, v, batch)
    out = _layer_tail(
        o, x1,
        [bf(sb_w_out[0]), ln_mix_g[1], ln_mix_b[1],
         bf(mlp_w1[1]), mlp_b1[1], bf(mlp_w2[1]), mlp_b2[1], ln_mlp_g[1], ln_mlp_b[1]],
        glu=False)
    return out.reshape(batch, seq, d)
```

```python
import functools
import math

import jax
import jax.numpy as jnp
from jax import lax
from jax.experimental import pallas as pl
from jax.experimental.pallas import tpu as pltpu

F32 = jnp.float32
BF16 = jnp.bfloat16

D_MODEL = 1024
DEPTH = 2
S5_GROUP = 16
S5_GROUPS = D_MODEL // S5_GROUP
S5_STATE = 64
SB_HEADS = 16
SB_HEAD_DIM = D_MODEL // SB_HEADS
D_FF = 4 * D_MODEL
DEEPNORM_ALPHA = (2.0 * DEPTH) ** 0.25
LN_EPS = 1e-5

LANES = 128
SUBLANES = 8
KB = 16
GROUPS_PER_TILE = LANES // S5_GROUP
N_TILES = D_MODEL // LANES
TILE_STATE = GROUPS_PER_TILE * S5_STATE
ALL_STATE = S5_GROUPS * S5_STATE
VMEM_LIMIT = 56 * 1024 * 1024
TAIL_SPLIT = 2
ATT_BLOCK = 128
ATT_PAIRS = 8
KEY_NORM_UNROLL = 4
HEAD_ROWS = 64
HEAD_KEYS = 64
EXP2_FLOOR = -130.0
BOUNDED_SCORE = 120.0
LOG2_E = 1.0 / math.log(2.0)
NT_DIMS = (((1,), (1,)), ((), ()))


def _params(*sem):
    return pltpu.CompilerParams(dimension_semantics=sem, vmem_limit_bytes=VMEM_LIMIT)


def _resident(shape):
    zeros = (0,) * len(shape)
    return pl.BlockSpec(shape, lambda *_: zeros, pipeline_mode=pl.Buffered(1))


def _cmul(ar, ai, br, bi):
    return ar * br - ai * bi, ar * bi + ai * br


def _s5_prep_kernel(lr_ref, li_ref, ls_ref, br_ref, bi_ref, cr_ref, ci_ref,
                    car_ref, cai_ref, wr_ref, wi_ref, pwr_ref, pwi_ref,
                    a1r_ref, a1i_ref, pr_ref, pi_ref, bbr_ref, bbi_ref):
    tau = pl.program_id(0)

    @pl.when(tau == 0)
    def _():
        lr = lr_ref[...]
        li = li_ref[...]
        dt = jnp.exp(ls_ref[...])
        mag = jnp.exp(lr * dt)
        ang = li * dt
        a_re = mag * jnp.cos(ang)
        a_im = mag * jnp.sin(ang)
        nr = a_re - 1.0
        ni = a_im
        den = lr * lr + li * li
        f_re = (nr * lr + ni * li) / den
        f_im = (ni * lr - nr * li) / den
        bbr, bbi = _cmul(f_re, f_im, br_ref[...], bi_ref[...])
        bbr_ref[...] = bbr
        bbi_ref[...] = bbi
        a1r_ref[...] = a_re
        a1i_ref[...] = a_im
        pr_ref[...] = jnp.ones_like(a_re)
        pi_ref[...] = jnp.zeros_like(a_im)

    p_re = pr_ref[...]
    p_im = pi_ref[...]
    car, cai = _cmul(cr_ref[...], ci_ref[...], p_re, p_im)
    car_ref[...] = car
    cai_ref[...] = -cai
    wr, wi = _cmul(p_re, p_im, bbr_ref[...], bbi_ref[...])
    wr_ref[...] = wr
    wi_ref[...] = wi

    @pl.when(tau == KB)
    def _():
        q_re, q_im = p_re, p_im
        for i in range(SUBLANES):
            pwr_ref[i:i + 1, :] = q_re
            pwi_ref[i:i + 1, :] = q_im
            q_re, q_im = _cmul(q_re, q_im, p_re, p_im)

    n_re, n_im = _cmul(p_re, p_im, a1r_ref[...], a1i_ref[...])
    pr_ref[...] = n_re
    pi_ref[...] = n_im


def _s5_prep(lam_re, lam_im, log_step, b_re, b_im, c_re, c_im):
    g, p, h = S5_GROUPS, S5_STATE, S5_GROUP
    lr = lam_re.reshape(1, ALL_STATE)
    li = lam_im.reshape(1, ALL_STATE)
    ls = jnp.broadcast_to(log_step[:, None], (g, p)).reshape(1, ALL_STATE)
    br = b_re.transpose(2, 0, 1).reshape(h, ALL_STATE)
    bi = b_im.transpose(2, 0, 1).reshape(h, ALL_STATE)
    cr = c_re.transpose(1, 0, 2).reshape(h, ALL_STATE)
    ci = c_im.transpose(1, 0, 2).reshape(h, ALL_STATE)
    small = pl.BlockSpec((1, ALL_STATE), lambda t: (0, 0))
    big = pl.BlockSpec((h, ALL_STATE), lambda t: (0, 0))
    step = pl.BlockSpec((None, h, ALL_STATE), lambda t: (t, 0, 0))
    pw = pl.BlockSpec((SUBLANES, ALL_STATE), lambda t: (0, 0))
    out3 = jax.ShapeDtypeStruct((KB + 1, h, ALL_STATE), F32)
    outp = jax.ShapeDtypeStruct((SUBLANES, ALL_STATE), F32)
    return pl.pallas_call(
        _s5_prep_kernel,
        grid=(KB + 1,),
        in_specs=[small, small, small, big, big, big, big],
        out_specs=[step, step, step, step, pw, pw],
        out_shape=[out3, out3, out3, out3, outp, outp],
        scratch_shapes=[pltpu.VMEM((1, ALL_STATE), F32)] * 4 + [pltpu.VMEM((h, ALL_STATE), F32)] * 2,
        compiler_params=_params("arbitrary"),
        name="s5_prep",
    )(lr, li, ls, br, bi, cr, ci)


def _uproj_kernel(x_ref, w_ref, o_ref):
    o_ref[...] = jnp.dot(x_ref[...].astype(BF16), w_ref[...], preferred_element_type=F32)


def _u_proj(x, w_in):
    n = x.shape[0]
    tm = min(n, 1024)
    row = pl.BlockSpec((tm, D_MODEL), lambda i: (i, 0))
    return pl.pallas_call(
        _uproj_kernel,
        grid=(n // tm,),
        in_specs=[row, _resident((D_MODEL, D_MODEL))],
        out_specs=row,
        out_shape=jax.ShapeDtypeStruct((n, D_MODEL), F32),
        compiler_params=_params("parallel"),
        name="s5_u_proj",
    )(x, w_in)


def _shift_rows(x, k, row):
    return jnp.where(row >= k, pltpu.roll(x, k, 0), 0.0)


def _ssm_build_operators(car_ref, cai_ref, wr_ref, wi_ref, t_ref, win_ref, voutt_ref):
    ts = TILE_STATE
    row_group = lax.broadcasted_iota(jnp.int32, (LANES, ts), 0) // S5_GROUP
    lane_group = lax.broadcasted_iota(jnp.int32, (LANES, ts), 1) // S5_STATE
    same_group = row_group == lane_group

    def expand(re16, im16):
        tile = lambda a: jnp.where(same_group, jnp.concatenate([a] * GROUPS_PER_TILE, axis=0), 0.0)
        return jnp.concatenate([tile(re16), tile(im16)], axis=1)

    def split(a):
        hi = a.astype(BF16)
        return hi, (a - hi.astype(F32)).astype(BF16)

    def dot_nt(a, b):
        return lax.dot_general(a, b, NT_DIMS, preferred_element_type=F32)

    t_ref[...] = jnp.zeros_like(t_ref)
    bbar_hi, bbar_lo = split(expand(wr_ref[0], wi_ref[0]))
    for tau in range(KB + 1):
        ca_hi, ca_lo = split(expand(car_ref[tau], cai_ref[tau]))
        if tau >= 1:
            voutt_ref[(tau - 1) * LANES:tau * LANES, :] = ca_hi
        if tau < KB:
            lag = (dot_nt(bbar_hi, ca_hi) + dot_nt(bbar_hi, ca_lo) + dot_nt(bbar_lo, ca_hi)).astype(BF16)
            for j in range(KB - tau):
                t_ref[j * LANES:(j + 1) * LANES, (j + tau) * LANES:(j + tau + 1) * LANES] = lag
            win_ref[(KB - 1 - tau) * LANES:(KB - tau) * LANES, :] = expand(wr_ref[tau], wi_ref[tau]).astype(BF16)


def _ssm_kernel(u_ref, car_ref, cai_ref, wr_ref, wi_ref, pwr_ref, pwi_ref, d_ref, y_ref,
                s_ref, t_ref, win_ref, voutt_ref):
    nb = u_ref.shape[0] // KB
    ts = TILE_STATE

    @pl.when(pl.program_id(1) == 0)
    def _():
        _ssm_build_operators(car_ref, cai_ref, wr_ref, wi_ref, t_ref, win_ref, voutt_ref)

    u_cat = jnp.concatenate([u_ref[pl.ds(j, nb, stride=KB), :] for j in range(KB)], axis=1)
    u_bf = u_cat.astype(BF16)
    s_ref[...] = jnp.dot(u_bf, win_ref[...], preferred_element_type=F32)
    row = lax.broadcasted_iota(jnp.int32, (SUBLANES, ts), 0)
    pw_re = pwr_ref[...]
    pw_im = pwi_ref[...]

    def group(r, carry):
        c_re, c_im = carry
        rows = pl.ds(pl.multiple_of(r * SUBLANES, SUBLANES), SUBLANES)
        x_re = s_ref[rows, 0:ts]
        x_im = s_ref[rows, ts:2 * ts]
        for k in (1, 2, 4):
            a_re = pw_re[k - 1:k]
            a_im = pw_im[k - 1:k]
            sh_re = _shift_rows(x_re, k, row)
            sh_im = _shift_rows(x_im, k, row)
            x_re, x_im = x_re + a_re * sh_re - a_im * sh_im, x_im + a_re * sh_im + a_im * sh_re
        x_re, x_im = x_re + pw_re * c_re - pw_im * c_im, x_im + pw_re * c_im + pw_im * c_re
        s_ref[rows, 0:ts] = jnp.where(row >= 1, pltpu.roll(x_re, 1, 0), c_re)
        s_ref[rows, ts:2 * ts] = jnp.where(row >= 1, pltpu.roll(x_im, 1, 0), c_im)
        return x_re[SUBLANES - 1:SUBLANES], x_im[SUBLANES - 1:SUBLANES]

    zero = jnp.zeros((1, ts), F32)
    lax.fori_loop(0, nb // SUBLANES, group, (zero, zero))
    s_bf = s_ref[...].astype(BF16)
    width = 2 * LANES
    for c in range(KB * LANES // width):
        cols = slice(c * width, (c + 1) * width)
        live = (c + 1) * width
        y = jnp.dot(u_bf[:, :live], t_ref[0:live, cols], preferred_element_type=F32)
        y = y + lax.dot_general(s_bf, voutt_ref[cols, :], NT_DIMS, preferred_element_type=F32)
        y = y + d_ref[:, cols] * u_cat[:, cols]
        for jj in range(width // LANES):
            y_ref[pl.ds(c * (width // LANES) + jj, nb, stride=KB), :] = y[:, jj * LANES:(jj + 1) * LANES]


def _ssm(u, car, cai, wr, wi, pwr, pwi, d_tiled, batch):
    seq = u.shape[0] // batch
    nb = seq // KB
    factor = pl.BlockSpec((KB + 1, S5_GROUP, TILE_STATE), lambda c, b: (0, 0, c))
    power = pl.BlockSpec((SUBLANES, TILE_STATE), lambda c, b: (0, c))
    io = pl.BlockSpec((seq, LANES), lambda c, b: (b, c))
    return pl.pallas_call(
        _ssm_kernel,
        grid=(N_TILES, batch),
        in_specs=[io, factor, factor, factor, factor, power, power,
                  pl.BlockSpec((None, 1, KB * LANES), lambda c, b: (c, 0, 0))],
        out_specs=io,
        out_shape=jax.ShapeDtypeStruct(u.shape, F32),
        scratch_shapes=[pltpu.VMEM((nb, 2 * TILE_STATE), F32),
                        pltpu.VMEM((KB * LANES, KB * LANES), BF16),
                        pltpu.VMEM((KB * LANES, 2 * TILE_STATE), BF16),
                        pltpu.VMEM((KB * LANES, 2 * TILE_STATE), BF16)],
        compiler_params=_params("parallel", "arbitrary"),
        name="s5_ssm",
    )(u, car, cai, wr, wi, pwr, pwi, d_tiled)


def _layer_norm(z, g, b):
    mu = jnp.mean(z, axis=-1, keepdims=True)
    zc = z - mu
    var = jnp.mean(zc * zc, axis=-1, keepdims=True)
    return zc * lax.rsqrt(var + LN_EPS) * g + b


def _tail_kernel(*refs, glu):
    if glu:
        (m_ref, x_ref, wglu_ref, bglu_ref, wout_ref, g1_ref, b1n_ref,
         w1_ref, b1_ref, w2_ref, b2_ref, g2_ref, b2n_ref, o_ref) = refs
    else:
        (m_ref, x_ref, wout_ref, g1_ref, b1n_ref,
         w1_ref, b1_ref, w2_ref, b2_ref, g2_ref, b2n_ref, o_ref) = refs
    tm = x_ref.shape[0]
    parts = [slice(h * tm // TAIL_SPLIT, (h + 1) * tm // TAIL_SPLIT) for h in range(TAIL_SPLIT)]
    mm = lambda a, b: jnp.dot(a, b, preferred_element_type=F32)
    if glu:
        g = [jax.nn.gelu(m_ref[rows, :]) for rows in parts]
        gate = [mm(gh.astype(BF16), wglu_ref[...]) + bglu_ref[...] for gh in g]
        m = [(gh * jax.nn.sigmoid(th)).astype(BF16) for gh, th in zip(g, gate)]
    else:
        m = [m_ref[rows, :] for rows in parts]
    mix = [mm(mh, wout_ref[...]) for mh in m]
    x1 = [_layer_norm(DEEPNORM_ALPHA * x_ref[rows, :] + mh, g1_ref[...], b1n_ref[...])
          for rows, mh in zip(parts, mix)]
    x1_bf = [a.astype(BF16) for a in x1]
    ff = [None] * TAIL_SPLIT
    for c in range(D_FF // D_MODEL):
        cols = slice(c * D_MODEL, (c + 1) * D_MODEL)
        hid = [jnp.square(jnp.maximum(mm(a, w1_ref[:, cols]) + b1_ref[:, cols], 0.0)) for a in x1_bf]
        for h in range(TAIL_SPLIT):
            out = mm(hid[h].astype(BF16), w2_ref[cols, :])
            ff[h] = out if c == 0 else ff[h] + out
    for h, rows in enumerate(parts):
        o_ref[rows, :] = _layer_norm(DEEPNORM_ALPHA * x1[h] + ff[h] + b2_ref[...], g2_ref[...], b2n_ref[...])


def _layer_tail(m, x, weights, glu):
    n = x.shape[0]
    tm = min(n, 512)
    row = pl.BlockSpec((tm, D_MODEL), lambda i: (i, 0))
    ops = [m, x]
    specs = [row, row]
    for wgt in weights:
        a = wgt.reshape(1, -1) if wgt.ndim == 1 else wgt
        ops.append(a)
        specs.append(_resident(a.shape))
    return pl.pallas_call(
        functools.partial(_tail_kernel, glu=glu),
        grid=(n // tm,),
        in_specs=specs,
        out_specs=row,
        out_shape=jax.ShapeDtypeStruct((n, D_MODEL), F32),
        compiler_params=_params("parallel"),
        name="layer_tail_glu" if glu else "layer_tail",
    )(*ops)


def _qkv_kernel(x_ref, w_ref, q_ref, k_ref, v_ref):
    y = jnp.dot(x_ref[...].astype(BF16), w_ref[...], preferred_element_type=F32)
    q_ref[...] = (y[:, :D_MODEL] * (LOG2_E / math.sqrt(SB_HEAD_DIM))).astype(BF16)
    k_ref[...] = y[:, D_MODEL:2 * D_MODEL].astype(BF16)
    v_ref[...] = y[:, 2 * D_MODEL:].astype(BF16)


def _qkv(x, w_qkv):
    n = x.shape[0]
    tm = min(n, 512)
    row = pl.BlockSpec((tm, D_MODEL), lambda i: (i, 0))
    out = jax.ShapeDtypeStruct((n, D_MODEL), BF16)
    return pl.pallas_call(
        _qkv_kernel,
        grid=(n // tm,),
        in_specs=[row, _resident(w_qkv.shape)],
        out_specs=[row, row, row],
        out_shape=[out, out, out],
        compiler_params=_params("parallel"),
        name="qkv_proj",
    )(x, w_qkv)


def _attn_kernel(q_ref, k_ref, v_ref, ntri_ref, ntrih_ref, hsum_ref, o_ref, acc_ref, r_ref, st_ref):
    i = pl.program_id(2)
    tb = ATT_BLOCK
    pairs = q_ref.shape[1] // LANES
    lane = lax.broadcasted_iota(jnp.int32, (tb, LANES), 1)
    first = lane < SB_HEAD_DIM

    def split_heads(a):
        zero = jnp.zeros_like(a)
        mine = lax.broadcasted_iota(jnp.int32, a.shape, 1) < SB_HEAD_DIM
        return jnp.concatenate([jnp.where(mine, a, zero), jnp.where(mine, zero, a)], axis=0)

    def head_sq_norms(a):
        sq = jnp.square(a.astype(F32)).astype(BF16)
        return jnp.dot(sq, hsum_ref[...], preferred_element_type=F32)

    def pair_max_sq_norm(ref, rows):
        m = head_sq_norms(ref[rows, 0:LANES])
        for p in range(1, pairs):
            m = jnp.maximum(m, head_sq_norms(ref[rows, p * LANES:(p + 1) * LANES]))
        return m

    @pl.when(i == 0)
    def _():
        def part(c, m):
            for sub in range(KEY_NORM_UNROLL):
                rows = pl.ds(pl.multiple_of((c * KEY_NORM_UNROLL + sub) * tb, tb), tb)
                m = jnp.maximum(m, pair_max_sq_norm(k_ref, rows))
            return m
        trips = k_ref.shape[0] // (tb * KEY_NORM_UNROLL)
        st_ref[0] = jnp.max(lax.fori_loop(0, trips, part, jnp.zeros((tb, LANES), F32)))

    acc_ref[...] = jnp.zeros_like(acc_ref)
    r_ref[...] = jnp.zeros_like(r_ref)
    rr = lax.broadcasted_iota(jnp.int32, (tb, 2 * tb), 0)
    cc = lax.broadcasted_iota(jnp.int32, (tb, 2 * tb), 1)
    causal = jnp.bitwise_and(cc, tb - 1) < rr
    lanes = [slice(p * LANES, (p + 1) * LANES) for p in range(pairs)]

    hr, hk = HEAD_ROWS, HEAD_KEYS
    late = jnp.logical_not(jnp.logical_and(rr < hr, jnp.bitwise_and(cc, tb - 1) >= tb - hk))
    lane_h = lax.broadcasted_iota(jnp.int32, (hr, 2 * hk), 1)
    head_h = [lane_h < hk, lane_h >= hk]

    def cost_of(z, no_overflow):
        if no_overflow:
            return jnp.log2(1.0 + jnp.exp2(z))
        return jnp.maximum(z, 0.0) + jnp.log2(1.0 + 1.0 / jnp.exp2(jnp.abs(z)))

    def hi_lo(c):
        hi = c.astype(BF16)
        return jnp.concatenate([hi, (c - hi.astype(F32)).astype(BF16)], axis=1)

    def visit(blocks, masks, first=False, head_block=None):
        keys = [pl.ds(pl.multiple_of(j * tb, tb), tb) for j in blocks]
        todo = [(p, t) for p in range(pairs) for t in range(len(blocks))]
        z = {pt: lax.dot_general(q_ref[:, lanes[pt[0]]], split_heads(k_ref[keys[pt[1]], lanes[pt[0]]]), NT_DIMS,
                                 preferred_element_type=F32) for pt in todo}
        if head_block is not None:
            hkeys = pl.ds(pl.multiple_of(head_block * tb + (tb - hk), hk), hk)
            zh = [lax.dot_general(q_ref[0:hr, lanes[p]], split_heads(k_ref[hkeys, lanes[p]]), NT_DIMS,
                                  preferred_element_type=F32) for p in range(pairs)]

        def finish(no_overflow):
            cost = {}
            for pt in todo:
                c = cost_of(z[pt], no_overflow)
                cost[pt] = c if masks[pt[1]] is None else jnp.where(masks[pt[1]], c, 0.0)
            suffix = {pt: jnp.dot(hi_lo(cost[pt]), ntri_ref[...], preferred_element_type=F32) for pt in todo}
            if head_block is not None:
                cost_h = [cost_of(zh[p], no_overflow) for p in range(pairs)]
                suffix_h = [jnp.dot(hi_lo(cost_h[p]), ntrih_ref[...], preferred_element_type=F32)
                            for p in range(pairs)]
            r_top = None
            for p in range(pairs):
                r = r_ref[p]
                weights = []
                for t in range(len(blocks)):
                    w = jnp.exp2(z[p, t] + suffix[p, t] + r)
                    if masks[t] is not None:
                        w = jnp.where(masks[t], w, 0.0)
                    weights.append(w.astype(BF16))
                    spent = [jnp.broadcast_to(jnp.sum(cost[p, t][:, h * tb:(h + 1) * tb], axis=1, keepdims=True),
                                              (tb, tb)) for h in range(2)]
                    r = r - jnp.concatenate(spent, axis=1)
                values = jnp.concatenate([split_heads(v_ref[keys[t], lanes[p]]) for t in range(len(blocks))], axis=0)
                acc_ref[:, lanes[p]] += jnp.dot(jnp.concatenate(weights, axis=1), values,
                                                preferred_element_type=F32)
                if head_block is not None:
                    r_h = r[0:hr]
                    w = jnp.exp2(zh[p] + suffix_h[p] + jnp.where(head_h[0], r_h[:, 0:LANES], r_h[:, tb:tb + LANES]))
                    acc_ref[0:hr, lanes[p]] += jnp.dot(w.astype(BF16), split_heads(v_ref[hkeys, lanes[p]]),
                                                       preferred_element_type=F32)
                    spent = [jnp.broadcast_to(jnp.sum(jnp.where(head_h[h], cost_h[p], 0.0), axis=1, keepdims=True),
                                              (hr, tb)) for h in range(2)]
                    r = jnp.concatenate([r_h - jnp.concatenate(spent, axis=1), r[hr:]], axis=0)
                r_ref[p] = r
                r_top = r if r_top is None else jnp.maximum(r_top, r)
            st_ref[1] = jnp.max(r_top)

        if first:
            qn = jnp.max(pair_max_sq_norm(q_ref, slice(None)))
            st_ref[2] = jnp.sqrt(qn * st_ref[0]) * 1.02 + 1e-3
        if first and len(blocks) > 1:
            bounded = st_ref[2] < BOUNDED_SCORE
            pl.when(bounded)(functools.partial(finish, True))
            pl.when(jnp.logical_not(bounded))(functools.partial(finish, False))
        else:
            finish(False)
        return st_ref[1]

    def unfinished():
        return st_ref[1] + st_ref[2] > EXP2_FLOOR

    has_third = i >= 2

    @pl.when(has_third)
    def _():
        visit([i, i - 1], [causal, None], first=True, head_block=i - 2)

    @pl.when(jnp.logical_and(has_third, unfinished()))
    def _():
        visit([i - 2], [late])

    @pl.when(jnp.logical_not(has_third))
    def _():
        visit([i], [causal], first=True)

    def more(c):
        j, r_max = c
        return jnp.logical_and(j >= 0, r_max + st_ref[2] > EXP2_FLOOR)

    def step(c):
        j, _ = c
        return j - 1, visit([j], [None])

    lax.while_loop(more, step, (jnp.where(has_third, i - 3, i - 1), st_ref[1]))
    o_ref[...] = acc_ref[...].astype(o_ref.dtype)


def _attention(q, k, v, batch):
    n = q.shape[0]
    seq = n // batch
    tb = ATT_BLOCK
    nq = seq // tb
    width = ATT_PAIRS * LANES
    idx = jnp.arange(2 * tb)
    same_head = (idx[:, None] // tb) == (idx[None, :] // tb)
    ntri = -(same_head & (idx[:, None] >= idx[None, :])).astype(BF16)
    ntri = jnp.concatenate([ntri, ntri], axis=0)
    idh = jnp.arange(2 * HEAD_KEYS)
    ntrih = -((idh[:, None] // HEAD_KEYS == idh[None, :] // HEAD_KEYS) & (idh[:, None] >= idh[None, :])).astype(BF16)
    ntrih = jnp.concatenate([ntrih, ntrih], axis=0)
    hd = jnp.arange(LANES) // SB_HEAD_DIM
    hsum = (hd[:, None] == hd[None, :]).astype(BF16)
    kv_spec = pl.BlockSpec((seq, width), lambda b, h, i: (b, h))
    q_spec = pl.BlockSpec((tb, width), lambda b, h, i: (b * nq + i, h))
    return pl.pallas_call(
        _attn_kernel,
        grid=(batch, D_MODEL // width, nq),
        in_specs=[q_spec, kv_spec, kv_spec, _resident(ntri.shape), _resident(ntrih.shape), _resident(hsum.shape)],
        out_specs=q_spec,
        out_shape=jax.ShapeDtypeStruct((n, D_MODEL), BF16),
        scratch_shapes=[pltpu.VMEM((tb, width), F32), pltpu.VMEM((ATT_PAIRS, tb, 2 * tb), F32),
                        pltpu.SMEM((3,), F32)],
        compiler_params=_params("parallel", "parallel", "arbitrary"),
        name="sb_attention",
    )(q, k, v, ntri, ntrih, hsum)


def kernel(x, s5_w_in, s5_lambda_re, s5_lambda_im, s5_b_re, s5_b_im, s5_c_re, s5_c_im, s5_d, s5_log_step,
           s5_w_glu, s5_b_glu, s5_w_out, sb_w_kv, sb_w_q, sb_w_out, mlp_w1, mlp_b1, mlp_w2, mlp_b2,
           ln_mix_g, ln_mix_b, ln_mlp_g, ln_mlp_b):
    batch, seq, d = x.shape
    n = batch * seq
    bf = lambda a: a.astype(BF16)
    x0 = x.reshape(n, d)

    car, cai, wr, wi, pwr, pwi = _s5_prep(s5_lambda_re[0], s5_lambda_im[0], s5_log_step[0],
                                          s5_b_re[0], s5_b_im[0], s5_c_re[0], s5_c_im[0])
    d_tiled = jnp.tile(s5_d[0].reshape(N_TILES, 1, LANES), (1, 1, KB))
    u = _u_proj(x0, bf(s5_w_in[0]))
    y = _ssm(u, car, cai, wr, wi, pwr, pwi, d_tiled, batch)
    x1 = _layer_tail(
        y, x0,
        [bf(s5_w_glu[0]), s5_b_glu[0], bf(s5_w_out[0]), ln_mix_g[0], ln_mix_b[0],
         bf(mlp_w1[0]), mlp_b1[0], bf(mlp_w2[0]), mlp_b2[0], ln_mlp_g[0], ln_mlp_b[0]],
        glu=True)

    q, k, v = _qkv(x1, bf(jnp.concatenate([sb_w_q[0], sb_w_kv], axis=1)))
    o = _attention(q, k, v, batch)
    out = _layer_tail(
        o, x1,
        [bf(sb_w_out[0]), ln_mix_g[1], ln_mix_b[1],
         bf(mlp_w1[1]), mlp_b1[1], bf(mlp_w2[1]), mlp_b2[1], ln_mlp_g[1], ln_mlp_b[1]],
        glu=False)
    return out.reshape(batch, seq, d)
```

```python
import functools
import math

import jax
import jax.numpy as jnp
from jax import lax
from jax.experimental import pallas as pl
from jax.experimental.pallas import tpu as pltpu

F32 = jnp.float32
BF16 = jnp.bfloat16

D_MODEL = 1024
DEPTH = 2
S5_GROUP = 16
S5_GROUPS = D_MODEL // S5_GROUP
S5_STATE = 64
SB_HEADS = 16
SB_HEAD_DIM = D_MODEL // SB_HEADS
D_FF = 4 * D_MODEL
DEEPNORM_ALPHA = (2.0 * DEPTH) ** 0.25
LN_EPS = 1e-5

LANES = 128
SUBLANES = 8
KB = 16
GROUPS_PER_TILE = LANES // S5_GROUP
N_TILES = D_MODEL // LANES
TILE_STATE = GROUPS_PER_TILE * S5_STATE
ALL_STATE = S5_GROUPS * S5_STATE
VMEM_LIMIT = 56 * 1024 * 1024
TAIL_SPLIT = 2
ATT_BLOCK = 128
ATT_PAIRS = 8
KEY_NORM_UNROLL = 4
HEAD_ROWS = 64
HEAD_KEYS = 64
EXP2_FLOOR = -130.0
BOUNDED_SCORE = 120.0
LOG2_E = 1.0 / math.log(2.0)
NT_DIMS = (((1,), (1,)), ((), ()))


def _params(*sem):
    return pltpu.CompilerParams(dimension_semantics=sem, vmem_limit_bytes=VMEM_LIMIT)


def _resident(shape):
    zeros = (0,) * len(shape)
    return pl.BlockSpec(shape, lambda *_: zeros, pipeline_mode=pl.Buffered(1))


def _cmul(ar, ai, br, bi):
    return ar * br - ai * bi, ar * bi + ai * br


def _s5_prep_kernel(lr_ref, li_ref, ls_ref, br_ref, bi_ref, cr_ref, ci_ref,
                    car_ref, cai_ref, wr_ref, wi_ref, pwr_ref, pwi_ref,
                    a1r_ref, a1i_ref, pr_ref, pi_ref, bbr_ref, bbi_ref):
    tau = pl.program_id(0)

    @pl.when(tau == 0)
    def _():
        lr = lr_ref[...]
        li = li_ref[...]
        dt = jnp.exp(ls_ref[...])
        mag = jnp.exp(lr * dt)
        ang = li * dt
        a_re = mag * jnp.cos(ang)
        a_im = mag * jnp.sin(ang)
        nr = a_re - 1.0
        ni = a_im
        den = lr * lr + li * li
        f_re = (nr * lr + ni * li) / den
        f_im = (ni * lr - nr * li) / den
        bbr, bbi = _cmul(f_re, f_im, br_ref[...], bi_ref[...])
        bbr_ref[...] = bbr
        bbi_ref[...] = bbi
        a1r_ref[...] = a_re
        a1i_ref[...] = a_im
        pr_ref[...] = jnp.ones_like(a_re)
        pi_ref[...] = jnp.zeros_like(a_im)

    p_re = pr_ref[...]
    p_im = pi_ref[...]
    car, cai = _cmul(cr_ref[...], ci_ref[...], p_re, p_im)
    car_ref[...] = car
    cai_ref[...] = -cai
    wr, wi = _cmul(p_re, p_im, bbr_ref[...], bbi_ref[...])
    wr_ref[...] = wr
    wi_ref[...] = wi

    @pl.when(tau == KB)
    def _():
        q_re, q_im = p_re, p_im
        for i in range(SUBLANES):
            pwr_ref[i:i + 1, :] = q_re
            pwi_ref[i:i + 1, :] = q_im
            q_re, q_im = _cmul(q_re, q_im, p_re, p_im)

    n_re, n_im = _cmul(p_re, p_im, a1r_ref[...], a1i_ref[...])
    pr_ref[...] = n_re
    pi_ref[...] = n_im


def _s5_prep(lam_re, lam_im, log_step, b_re, b_im, c_re, c_im):
    g, p, h = S5_GROUPS, S5_STATE, S5_GROUP
    lr = lam_re.reshape(1, ALL_STATE)
    li = lam_im.reshape(1, ALL_STATE)
    ls = jnp.broadcast_to(log_step[:, None], (g, p)).reshape(1, ALL_STATE)
    br = b_re.transpose(2, 0, 1).reshape(h, ALL_STATE)
    bi = b_im.transpose(2, 0, 1).reshape(h, ALL_STATE)
    cr = c_re.transpose(1, 0, 2).reshape(h, ALL_STATE)
    ci = c_im.transpose(1, 0, 2).reshape(h, ALL_STATE)
    small = pl.BlockSpec((1, ALL_STATE), lambda t: (0, 0))
    big = pl.BlockSpec((h, ALL_STATE), lambda t: (0, 0))
    step = pl.BlockSpec((None, h, ALL_STATE), lambda t: (t, 0, 0))
    pw = pl.BlockSpec((SUBLANES, ALL_STATE), lambda t: (0, 0))
    out3 = jax.ShapeDtypeStruct((KB + 1, h, ALL_STATE), F32)
    outp = jax.ShapeDtypeStruct((SUBLANES, ALL_STATE), F32)
    return pl.pallas_call(
        _s5_prep_kernel,
        grid=(KB + 1,),
        in_specs=[small, small, small, big, big, big, big],
        out_specs=[step, step, step, step, pw, pw],
        out_shape=[out3, out3, out3, out3, outp, outp],
        scratch_shapes=[pltpu.VMEM((1, ALL_STATE), F32)] * 4 + [pltpu.VMEM((h, ALL_STATE), F32)] * 2,
        compiler_params=_params("arbitrary"),
        name="s5_prep",
    )(lr, li, ls, br, bi, cr, ci)


def _uproj_kernel(x_ref, w_ref, o_ref):
    o_ref[...] = jnp.dot(x_ref[...].astype(BF16), w_ref[...], preferred_element_type=F32)


def _u_proj(x, w_in):
    n = x.shape[0]
    tm = min(n, 1024)
    row = pl.BlockSpec((tm, D_MODEL), lambda i: (i, 0))
    return pl.pallas_call(
        _uproj_kernel,
        grid=(n // tm,),
        in_specs=[row, _resident((D_MODEL, D_MODEL))],
        out_specs=row,
        out_shape=jax.ShapeDtypeStruct((n, D_MODEL), F32),
        compiler_params=_params("parallel"),
        name="s5_u_proj",
    )(x, w_in)


def _shift_rows(x, k, row):
    return jnp.where(row >= k, pltpu.roll(x, k, 0), 0.0)


def _ssm_build_operators(car_ref, cai_ref, wr_ref, wi_ref, t_ref, win_ref, voutt_ref):
    ts = TILE_STATE
    row_group = lax.broadcasted_iota(jnp.int32, (LANES, ts), 0) // S5_GROUP
    lane_group = lax.broadcasted_iota(jnp.int32, (LANES, ts), 1) // S5_STATE
    same_group = row_group == lane_group

    def expand(re16, im16):
        tile = lambda a: jnp.where(same_group, jnp.concatenate([a] * GROUPS_PER_TILE, axis=0), 0.0)
        return jnp.concatenate([tile(re16), tile(im16)], axis=1)

    t_ref[...] = jnp.zeros_like(t_ref)
    bbar = expand(wr_ref[0], wi_ref[0]).astype(BF16)
    for tau in range(KB + 1):
        ca = expand(car_ref[tau], cai_ref[tau]).astype(BF16)
        if tau >= 1:
            voutt_ref[(tau - 1) * LANES:tau * LANES, :] = ca
        if tau < KB:
            lag = lax.dot_general(bbar, ca, NT_DIMS, preferred_element_type=F32).astype(BF16)
            for j in range(KB - tau):
                t_ref[j * LANES:(j + 1) * LANES, (j + tau) * LANES:(j + tau + 1) * LANES] = lag
            win_ref[(KB - 1 - tau) * LANES:(KB - tau) * LANES, :] = expand(wr_ref[tau], wi_ref[tau]).astype(BF16)


def _ssm_kernel(u_ref, car_ref, cai_ref, wr_ref, wi_ref, pwr_ref, pwi_ref, d_ref, y_ref,
                s_ref, yin_ref, t_ref, win_ref, voutt_ref):
    nb = u_ref.shape[0] // KB
    ts = TILE_STATE

    @pl.when(pl.program_id(1) == 0)
    def _():
        _ssm_build_operators(car_ref, cai_ref, wr_ref, wi_ref, t_ref, win_ref, voutt_ref)

    u_cat = jnp.concatenate([u_ref[pl.ds(j, nb, stride=KB), :] for j in range(KB)], axis=1)
    u_bf = u_cat.astype(BF16)
    s_ref[...] = jnp.dot(u_bf, win_ref[...], preferred_element_type=F32)
    width = 2 * LANES
    chunks = [slice(c * width, (c + 1) * width) for c in range(KB * LANES // width)]
    row = lax.broadcasted_iota(jnp.int32, (SUBLANES, ts), 0)
    pw_re = pwr_ref[...]
    pw_im = pwi_ref[...]

    def group(r, carry):
        c_re, c_im = carry
        rows = pl.ds(r * SUBLANES, SUBLANES)
        x_re = s_ref[rows, 0:ts]
        x_im = s_ref[rows, ts:2 * ts]
        for k in (1, 2, 4):
            a_re = pw_re[k - 1:k]
            a_im = pw_im[k - 1:k]
            sh_re = _shift_rows(x_re, k, row)
            sh_im = _shift_rows(x_im, k, row)
            x_re, x_im = x_re + a_re * sh_re - a_im * sh_im, x_im + a_re * sh_im + a_im * sh_re
        x_re, x_im = x_re + pw_re * c_re - pw_im * c_im, x_im + pw_re * c_im + pw_im * c_re
        s_ref[rows, 0:ts] = jnp.where(row >= 1, pltpu.roll(x_re, 1, 0), c_re)
        s_ref[rows, ts:2 * ts] = jnp.where(row >= 1, pltpu.roll(x_im, 1, 0), c_im)
        return x_re[SUBLANES - 1:SUBLANES], x_im[SUBLANES - 1:SUBLANES]

    groups = nb // SUBLANES
    carry = (jnp.zeros((1, ts), F32),) * 2
    for r in range(groups):
        for c, cols in enumerate(chunks):
            if c * groups // len(chunks) == r:
                yin_ref[:, cols] = jnp.dot(u_bf[:, :cols.stop], t_ref[0:cols.stop, cols],
                                           preferred_element_type=F32)
        carry = group(r, carry)
    s_bf = s_ref[...].astype(BF16)
    for c, cols in enumerate(chunks):
        y = yin_ref[:, cols] + lax.dot_general(s_bf, voutt_ref[cols, :], NT_DIMS, preferred_element_type=F32)
        y = y + d_ref[:, cols] * u_cat[:, cols]
        for jj in range(width // LANES):
            y_ref[pl.ds(c * (width // LANES) + jj, nb, stride=KB), :] = y[:, jj * LANES:(jj + 1) * LANES]


def _ssm(u, car, cai, wr, wi, pwr, pwi, d_tiled, batch):
    seq = u.shape[0] // batch
    nb = seq // KB
    factor = pl.BlockSpec((KB + 1, S5_GROUP, TILE_STATE), lambda c, b: (0, 0, c))
    power = pl.BlockSpec((SUBLANES, TILE_STATE), lambda c, b: (0, c))
    io = pl.BlockSpec((seq, LANES), lambda c, b: (b, c))
    return pl.pallas_call(
        _ssm_kernel,
        grid=(N_TILES, batch),
        in_specs=[io, factor, factor, factor, factor, power, power,
                  pl.BlockSpec((None, 1, KB * LANES), lambda c, b: (c, 0, 0))],
        out_specs=io,
        out_shape=jax.ShapeDtypeStruct(u.shape, F32),
        scratch_shapes=[pltpu.VMEM((nb, 2 * TILE_STATE), F32),
                        pltpu.VMEM((nb, KB * LANES), F32),
                        pltpu.VMEM((KB * LANES, KB * LANES), BF16),
                        pltpu.VMEM((KB * LANES, 2 * TILE_STATE), BF16),
                        pltpu.VMEM((KB * LANES, 2 * TILE_STATE), BF16)],
        compiler_params=_params("parallel", "arbitrary"),
        name="s5_ssm",
    )(u, car, cai, wr, wi, pwr, pwi, d_tiled)


def _layer_norm(z, g, b):
    mu = jnp.mean(z, axis=-1, keepdims=True)
    zc = z - mu
    var = jnp.mean(zc * zc, axis=-1, keepdims=True)
    return zc * lax.rsqrt(var + LN_EPS) * g + b


def _tail_kernel(*refs, glu):
    if glu:
        (m_ref, x_ref, wglu_ref, bglu_ref, wout_ref, g1_ref, b1n_ref,
         w1_ref, b1_ref, w2_ref, b2_ref, g2_ref, b2n_ref, o_ref) = refs
    else:
        (m_ref, x_ref, wout_ref, g1_ref, b1n_ref,
         w1_ref, b1_ref, w2_ref, b2_ref, g2_ref, b2n_ref, o_ref) = refs
    tm = x_ref.shape[0]
    parts = [slice(h * tm // TAIL_SPLIT, (h + 1) * tm // TAIL_SPLIT) for h in range(TAIL_SPLIT)]
    mm = lambda a, b: jnp.dot(a, b, preferred_element_type=F32)
    if glu:
        g = [jax.nn.gelu(m_ref[rows, :]) for rows in parts]
        gate = [mm(gh.astype(BF16), wglu_ref[...]) + bglu_ref[...] for gh in g]
        m = [(gh * jax.nn.sigmoid(th)).astype(BF16) for gh, th in zip(g, gate)]
    else:
        m = [m_ref[rows, :] for rows in parts]
    mix = [mm(mh, wout_ref[...]) for mh in m]
    x1 = [_layer_norm(DEEPNORM_ALPHA * x_ref[rows, :] + mh, g1_ref[...], b1n_ref[...])
          for rows, mh in zip(parts, mix)]
    x1_bf = [a.astype(BF16) for a in x1]
    ff = [None] * TAIL_SPLIT
    for c in range(D_FF // D_MODEL):
        cols = slice(c * D_MODEL, (c + 1) * D_MODEL)
        hid = [jnp.square(jnp.maximum(mm(a, w1_ref[:, cols]) + b1_ref[:, cols], 0.0)) for a in x1_bf]
        for h in range(TAIL_SPLIT):
            out = mm(hid[h].astype(BF16), w2_ref[cols, :])
            ff[h] = out if c == 0 else ff[h] + out
    for h, rows in enumerate(parts):
        o_ref[rows, :] = _layer_norm(DEEPNORM_ALPHA * x1[h] + ff[h] + b2_ref[...], g2_ref[...], b2n_ref[...])


def _layer_tail(m, x, weights, glu):
    n = x.shape[0]
    tm = min(n, 512)
    row = pl.BlockSpec((tm, D_MODEL), lambda i: (i, 0))
    ops = [m, x]
    specs = [row, row]
    for wgt in weights:
        a = wgt.reshape(1, -1) if wgt.ndim == 1 else wgt
        ops.append(a)
        specs.append(_resident(a.shape))
    return pl.pallas_call(
        functools.partial(_tail_kernel, glu=glu),
        grid=(n // tm,),
        in_specs=specs,
        out_specs=row,
        out_shape=jax.ShapeDtypeStruct((n, D_MODEL), F32),
        compiler_params=_params("parallel"),
        name="layer_tail_glu" if glu else "layer_tail",
    )(*ops)


def _qkv_kernel(x_ref, w_ref, q_ref, k_ref, v_ref):
    y = jnp.dot(x_ref[...].astype(BF16), w_ref[...], preferred_element_type=F32)
    q_ref[...] = (y[:, :D_MODEL] * (LOG2_E / math.sqrt(SB_HEAD_DIM))).astype(BF16)
    k_ref[...] = y[:, D_MODEL:2 * D_MODEL].astype(BF16)
    v_ref[...] = y[:, 2 * D_MODEL:].astype(BF16)


def _qkv(x, w_qkv):
    n = x.shape[0]
    tm = min(n, 512)
    row = pl.BlockSpec((tm, D_MODEL), lambda i: (i, 0))
    out = jax.ShapeDtypeStruct((n, D_MODEL), BF16)
    return pl.pallas_call(
        _qkv_kernel,
        grid=(n // tm,),
        in_specs=[row, _resident(w_qkv.shape)],
        out_specs=[row, row, row],
        out_shape=[out, out, out],
        compiler_params=_params("parallel"),
        name="qkv_proj",
    )(x, w_qkv)


def _attn_kernel(q_ref, k_ref, v_ref, ntri_ref, ntrih_ref, hsum_ref, o_ref, acc_ref, r_ref, st_ref):
    i = pl.program_id(2)
    tb = ATT_BLOCK
    pairs = q_ref.shape[1] // LANES
    lane = lax.broadcasted_iota(jnp.int32, (tb, LANES), 1)
    first = lane < SB_HEAD_DIM

    def split_heads(a):
        zero = jnp.zeros_like(a)
        mine = lax.broadcasted_iota(jnp.int32, a.shape, 1) < SB_HEAD_DIM
        return jnp.concatenate([jnp.where(mine, a, zero), jnp.where(mine, zero, a)], axis=0)

    def head_sq_norms(a):
        sq = jnp.square(a.astype(F32)).astype(BF16)
        return jnp.dot(sq, hsum_ref[...], preferred_element_type=F32)

    def pair_max_sq_norm(ref, rows):
        m = head_sq_norms(ref[rows, 0:LANES])
        for p in range(1, pairs):
            m = jnp.maximum(m, head_sq_norms(ref[rows, p * LANES:(p + 1) * LANES]))
        return m

    @pl.when(i == 0)
    def _():
        def part(c, m):
            for sub in range(KEY_NORM_UNROLL):
                rows = pl.ds(pl.multiple_of((c * KEY_NORM_UNROLL + sub) * tb, tb), tb)
                m = jnp.maximum(m, pair_max_sq_norm(k_ref, rows))
            return m
        trips = k_ref.shape[0] // (tb * KEY_NORM_UNROLL)
        st_ref[0] = jnp.max(lax.fori_loop(0, trips, part, jnp.zeros((tb, LANES), F32)))

    acc_ref[...] = jnp.zeros_like(acc_ref)
    r_ref[...] = jnp.zeros_like(r_ref)
    rr = lax.broadcasted_iota(jnp.int32, (tb, 2 * tb), 0)
    cc = lax.broadcasted_iota(jnp.int32, (tb, 2 * tb), 1)
    causal = jnp.bitwise_and(cc, tb - 1) < rr
    lanes = [slice(p * LANES, (p + 1) * LANES) for p in range(pairs)]

    hr, hk = HEAD_ROWS, HEAD_KEYS
    late = jnp.logical_not(jnp.logical_and(rr < hr, jnp.bitwise_and(cc, tb - 1) >= tb - hk))
    lane_h = lax.broadcasted_iota(jnp.int32, (hr, 2 * hk), 1)
    head_h = [lane_h < hk, lane_h >= hk]

    def cost_of(z, no_overflow):
        if no_overflow:
            return jnp.log2(1.0 + jnp.exp2(z))
        return jnp.maximum(z, 0.0) + jnp.log2(1.0 + 1.0 / jnp.exp2(jnp.abs(z)))

    def hi_lo(c):
        hi = c.astype(BF16)
        return jnp.concatenate([hi, (c - hi.astype(F32)).astype(BF16)], axis=1)

    def visit(blocks, masks, first=False, head_block=None):
        keys = [pl.ds(pl.multiple_of(j * tb, tb), tb) for j in blocks]
        todo = [(p, t) for p in range(pairs) for t in range(len(blocks))]
        z = {pt: lax.dot_general(q_ref[:, lanes[pt[0]]], split_heads(k_ref[keys[pt[1]], lanes[pt[0]]]), NT_DIMS,
                                 preferred_element_type=F32) for pt in todo}
        if head_block is not None:
            hkeys = pl.ds(pl.multiple_of(head_block * tb + (tb - hk), hk), hk)
            zh = [lax.dot_general(q_ref[0:hr, lanes[p]], split_heads(k_ref[hkeys, lanes[p]]), NT_DIMS,
                                  preferred_element_type=F32) for p in range(pairs)]

        def finish(no_overflow):
            cost = {}
            for pt in todo:
                c = cost_of(z[pt], no_overflow)
                cost[pt] = c if masks[pt[1]] is None else jnp.where(masks[pt[1]], c, 0.0)
            suffix = {pt: jnp.dot(hi_lo(cost[pt]), ntri_ref[...], preferred_element_type=F32) for pt in todo}
            if head_block is not None:
                cost_h = [cost_of(zh[p], no_overflow) for p in range(pairs)]
                suffix_h = [jnp.dot(hi_lo(cost_h[p]), ntrih_ref[...], preferred_element_type=F32)
                            for p in range(pairs)]
            r_top = None
            for p in range(pairs):
                r = r_ref[p]
                weights = []
                for t in range(len(blocks)):
                    w = jnp.exp2(z[p, t] + suffix[p, t] + r)
                    if masks[t] is not None:
                        w = jnp.where(masks[t], w, 0.0)
                    weights.append(w.astype(BF16))
                    spent = [jnp.broadcast_to(jnp.sum(cost[p, t][:, h * tb:(h + 1) * tb], axis=1, keepdims=True),
                                              (tb, tb)) for h in range(2)]
                    r = r - jnp.concatenate(spent, axis=1)
                values = jnp.concatenate([split_heads(v_ref[keys[t], lanes[p]]) for t in range(len(blocks))], axis=0)
                acc_ref[:, lanes[p]] += jnp.dot(jnp.concatenate(weights, axis=1), values,
                                                preferred_element_type=F32)
                if head_block is not None:
                    r_h = r[0:hr]
                    w = jnp.exp2(zh[p] + suffix_h[p] + jnp.where(head_h[0], r_h[:, 0:LANES], r_h[:, tb:tb + LANES]))
                    acc_ref[0:hr, lanes[p]] += jnp.dot(w.astype(BF16), split_heads(v_ref[hkeys, lanes[p]]),
                                                       preferred_element_type=F32)
                    spent = [jnp.broadcast_to(jnp.sum(jnp.where(head_h[h], cost_h[p], 0.0), axis=1, keepdims=True),
                                              (hr, tb)) for h in range(2)]
                    r = jnp.concatenate([r_h - jnp.concatenate(spent, axis=1), r[hr:]], axis=0)
                r_ref[p] = r
                r_top = r if r_top is None else jnp.maximum(r_top, r)
            st_ref[1] = jnp.max(r_top)

        if first:
            qn = jnp.max(pair_max_sq_norm(q_ref, slice(None)))
            st_ref[2] = jnp.sqrt(qn * st_ref[0]) * 1.02 + 1e-3
        if first and len(blocks) > 1:
            bounded = st_ref[2] < BOUNDED_SCORE
            pl.when(bounded)(functools.partial(finish, True))
            pl.when(jnp.logical_not(bounded))(functools.partial(finish, False))
        else:
            finish(False)
        return st_ref[1]

    def unfinished():
        return st_ref[1] + st_ref[2] > EXP2_FLOOR

    has_third = i >= 2

    @pl.when(has_third)
    def _():
        visit([i, i - 1], [causal, None], first=True, head_block=i - 2)

    @pl.when(jnp.logical_and(has_third, unfinished()))
    def _():
        visit([i - 2], [late])

    @pl.when(jnp.logical_not(has_third))
    def _():
        visit([i], [causal], first=True)

    def more(c):
        j, r_max = c
        return jnp.logical_and(j >= 0, r_max + st_ref[2] > EXP2_FLOOR)

    def step(c):
        j, _ = c
        return j - 1, visit([j], [None])

    lax.while_loop(more, step, (jnp.where(has_third, i - 3, i - 1), st_ref[1]))
    o_ref[...] = acc_ref[...].astype(o_ref.dtype)


def _attention(q, k, v, batch):
    n = q.shape[0]
    seq = n // batch
    tb = ATT_BLOCK
    nq = seq // tb
    width = ATT_PAIRS * LANES
    idx = jnp.arange(2 * tb)
    same_head = (idx[:, None] // tb) == (idx[None, :] // tb)
    ntri = -(same_head & (idx[:, None] >= idx[None, :])).astype(BF16)
    ntri = jnp.concatenate([ntri, ntri], axis=0)
    idh = jnp.arange(2 * HEAD_KEYS)
    ntrih = -((idh[:, None] // HEAD_KEYS == idh[None, :] // HEAD_KEYS) & (idh[:, None] >= idh[None, :])).astype(BF16)
    ntrih = jnp.concatenate([ntrih, ntrih], axis=0)
    hd = jnp.arange(LANES) // SB_HEAD_DIM
    hsum = (hd[:, None] == hd[None, :]).astype(BF16)
    kv_spec = pl.BlockSpec((seq, width), lambda b, h, i: (b, h))
    q_spec = pl.BlockSpec((tb, width), lambda b, h, i: (b * nq + i, h))
    return pl.pallas_call(
        _attn_kernel,
        grid=(batch, D_MODEL // width, nq),
        in_specs=[q_spec, kv_spec, kv_spec, _resident(ntri.shape), _resident(ntrih.shape), _resident(hsum.shape)],
        out_specs=q_spec,
        out_shape=jax.ShapeDtypeStruct((n, D_MODEL), BF16),
        scratch_shapes=[pltpu.VMEM((tb, width), F32), pltpu.VMEM((ATT_PAIRS, tb, 2 * tb), F32),
                        pltpu.SMEM((3,), F32)],
        compiler_params=_params("parallel", "parallel", "arbitrary"),
        name="sb_attention",
    )(q, k, v, ntri, ntrih, hsum)


def kernel(x, s5_w_in, s5_lambda_re, s5_lambda_im, s5_b_re, s5_b_im, s5_c_re, s5_c_im, s5_d, s5_log_step,
           s5_w_glu, s5_b_glu, s5_w_out, sb_w_kv, sb_w_q, sb_w_out, mlp_w1, mlp_b1, mlp_w2, mlp_b2,
           ln_mix_g, ln_mix_b, ln_mlp_g, ln_mlp_b):
    batch, seq, d = x.shape
    n = batch * seq
    bf = lambda a: a.astype(BF16)
    x0 = x.reshape(n, d)

    car, cai, wr, wi, pwr, pwi = _s5_prep(s5_lambda_re[0], s5_lambda_im[0], s5_log_step[0],
                                          s5_b_re[0], s5_b_im[0], s5_c_re[0], s5_c_im[0])
    d_tiled = jnp.tile(s5_d[0].reshape(N_TILES, 1, LANES), (1, 1, KB))
    u = _u_proj(x0, bf(s5_w_in[0]))
    y = _ssm(u, car, cai, wr, wi, pwr, pwi, d_tiled, batch)
    x1 = _layer_tail(
        y, x0,
        [bf(s5_w_glu[0]), s5_b_glu[0], bf(s5_w_out[0]), ln_mix_g[0], ln_mix_b[0],
         bf(mlp_w1[0]), mlp_b1[0], bf(mlp_w2[0]), mlp_b2[0], ln_mlp_g[0], ln_mlp_b[0]],
        glu=True)

    q, k, v = _qkv(x1, bf(jnp.concatenate([sb_w_q[0], sb_w_kv], axis=1)))
    o = _attention(q, k, v, batch)
    out = _layer_tail(
        o, x1,
        [bf(sb_w_out[0]), ln_mix_g[1], ln_mix_b[1],
         bf(mlp_w1[1]), mlp_b1[1], bf(mlp_w2[1]), mlp_b2[1], ln_mlp_g[1], ln_mlp_b[1]],
        glu=False)
    return out.reshape(batch, seq, d)
```

```python
import functools
import math

import jax
import jax.numpy as jnp
from jax import lax
from jax.experimental import pallas as pl
from jax.experimental.pallas import tpu as pltpu

F32 = jnp.float32
BF16 = jnp.bfloat16

D_MODEL = 1024
DEPTH = 2
S5_GROUP = 16
S5_GROUPS = D_MODEL // S5_GROUP
S5_STATE = 64
SB_HEADS = 16
SB_HEAD_DIM = D_MODEL // SB_HEADS
D_FF = 4 * D_MODEL
DEEPNORM_ALPHA = (2.0 * DEPTH) ** 0.25
LN_EPS = 1e-5

LANES = 128
SUBLANES = 8
KB = 16
GROUPS_PER_TILE = LANES // S5_GROUP
N_TILES = D_MODEL // LANES
TILE_STATE = GROUPS_PER_TILE * S5_STATE
ALL_STATE = S5_GROUPS * S5_STATE
VMEM_LIMIT = 56 * 1024 * 1024
TAIL_SPLIT = 2
ATT_BLOCK = 128
ATT_PAIRS = 8
QKV_ROWS = 512
HEAD_ROWS = 64
HEAD_KEYS = 64
EXP2_FLOOR = -130.0
BOUNDED_SCORE = 120.0
LOG2_E = 1.0 / math.log(2.0)
NT_DIMS = (((1,), (1,)), ((), ()))


def _params(*sem):
    return pltpu.CompilerParams(dimension_semantics=sem, vmem_limit_bytes=VMEM_LIMIT)


def _resident(shape):
    zeros = (0,) * len(shape)
    return pl.BlockSpec(shape, lambda *_: zeros, pipeline_mode=pl.Buffered(1))


def _cmul(ar, ai, br, bi):
    return ar * br - ai * bi, ar * bi + ai * br


def _s5_prep_kernel(lr_ref, li_ref, ls_ref, br_ref, bi_ref, cr_ref, ci_ref,
                    car_ref, cai_ref, wr_ref, wi_ref, pwr_ref, pwi_ref,
                    a1r_ref, a1i_ref, pr_ref, pi_ref, bbr_ref, bbi_ref):
    tau = pl.program_id(0)

    @pl.when(tau == 0)
    def _():
        lr = lr_ref[...]
        li = li_ref[...]
        dt = jnp.exp(ls_ref[...])
        mag = jnp.exp(lr * dt)
        ang = li * dt
        a_re = mag * jnp.cos(ang)
        a_im = mag * jnp.sin(ang)
        nr = a_re - 1.0
        ni = a_im
        den = lr * lr + li * li
        f_re = (nr * lr + ni * li) / den
        f_im = (ni * lr - nr * li) / den
        bbr, bbi = _cmul(f_re, f_im, br_ref[...], bi_ref[...])
        bbr_ref[...] = bbr
        bbi_ref[...] = bbi
        a1r_ref[...] = a_re
        a1i_ref[...] = a_im
        pr_ref[...] = jnp.ones_like(a_re)
        pi_ref[...] = jnp.zeros_like(a_im)

    p_re = pr_ref[...]
    p_im = pi_ref[...]
    car, cai = _cmul(cr_ref[...], ci_ref[...], p_re, p_im)
    car_ref[...] = car
    cai_ref[...] = -cai
    wr, wi = _cmul(p_re, p_im, bbr_ref[...], bbi_ref[...])
    wr_ref[...] = wr
    wi_ref[...] = wi

    @pl.when(tau == KB)
    def _():
        q_re, q_im = p_re, p_im
        for i in range(SUBLANES):
            pwr_ref[i:i + 1, :] = q_re
            pwi_ref[i:i + 1, :] = q_im
            q_re, q_im = _cmul(q_re, q_im, p_re, p_im)

    n_re, n_im = _cmul(p_re, p_im, a1r_ref[...], a1i_ref[...])
    pr_ref[...] = n_re
    pi_ref[...] = n_im


def _s5_prep(lam_re, lam_im, log_step, b_re, b_im, c_re, c_im):
    g, p, h = S5_GROUPS, S5_STATE, S5_GROUP
    lr = lam_re.reshape(1, ALL_STATE)
    li = lam_im.reshape(1, ALL_STATE)
    ls = jnp.broadcast_to(log_step[:, None], (g, p)).reshape(1, ALL_STATE)
    br = b_re.transpose(2, 0, 1).reshape(h, ALL_STATE)
    bi = b_im.transpose(2, 0, 1).reshape(h, ALL_STATE)
    cr = c_re.transpose(1, 0, 2).reshape(h, ALL_STATE)
    ci = c_im.transpose(1, 0, 2).reshape(h, ALL_STATE)
    small = pl.BlockSpec((1, ALL_STATE), lambda t: (0, 0))
    big = pl.BlockSpec((h, ALL_STATE), lambda t: (0, 0))
    step = pl.BlockSpec((None, h, ALL_STATE), lambda t: (t, 0, 0))
    pw = pl.BlockSpec((SUBLANES, ALL_STATE), lambda t: (0, 0))
    out3 = jax.ShapeDtypeStruct((KB + 1, h, ALL_STATE), F32)
    outp = jax.ShapeDtypeStruct((SUBLANES, ALL_STATE), F32)
    return pl.pallas_call(
        _s5_prep_kernel,
        grid=(KB + 1,),
        in_specs=[small, small, small, big, big, big, big],
        out_specs=[step, step, step, step, pw, pw],
        out_shape=[out3, out3, out3, out3, outp, outp],
        scratch_shapes=[pltpu.VMEM((1, ALL_STATE), F32)] * 4 + [pltpu.VMEM((h, ALL_STATE), F32)] * 2,
        compiler_params=_params("arbitrary"),
        name="s5_prep",
    )(lr, li, ls, br, bi, cr, ci)


def _uproj_kernel(x_ref, w_ref, o_ref):
    o_ref[...] = jnp.dot(x_ref[...].astype(BF16), w_ref[...], preferred_element_type=F32)


def _u_proj(x, w_in):
    n = x.shape[0]
    tm = min(n, 1024)
    row = pl.BlockSpec((tm, D_MODEL), lambda i: (i, 0))
    return pl.pallas_call(
        _uproj_kernel,
        grid=(n // tm,),
        in_specs=[row, _resident((D_MODEL, D_MODEL))],
        out_specs=row,
        out_shape=jax.ShapeDtypeStruct((n, D_MODEL), F32),
        compiler_params=_params("parallel"),
        name="s5_u_proj",
    )(x, w_in)


def _shift_rows(x, k, row):
    return jnp.where(row >= k, pltpu.roll(x, k, 0), 0.0)


def _ssm_build_operators(car_ref, cai_ref, wr_ref, wi_ref, t_ref, win_ref, voutt_ref):
    ts = TILE_STATE
    row_group = lax.broadcasted_iota(jnp.int32, (LANES, ts), 0) // S5_GROUP
    lane_group = lax.broadcasted_iota(jnp.int32, (LANES, ts), 1) // S5_STATE
    same_group = row_group == lane_group

    def expand(re16, im16):
        tile = lambda a: jnp.where(same_group, jnp.concatenate([a] * GROUPS_PER_TILE, axis=0), 0.0)
        return jnp.concatenate([tile(re16), tile(im16)], axis=1)

    t_ref[...] = jnp.zeros_like(t_ref)
    bbar = expand(wr_ref[0], wi_ref[0]).astype(BF16)
    for tau in range(KB + 1):
        ca = expand(car_ref[tau], cai_ref[tau]).astype(BF16)
        if tau >= 1:
            voutt_ref[(tau - 1) * LANES:tau * LANES, :] = ca
        if tau < KB:
            lag = lax.dot_general(bbar, ca, NT_DIMS, preferred_element_type=F32).astype(BF16)
            for j in range(KB - tau):
                t_ref[j * LANES:(j + 1) * LANES, (j + tau) * LANES:(j + tau + 1) * LANES] = lag
            win_ref[(KB - 1 - tau) * LANES:(KB - tau) * LANES, :] = expand(wr_ref[tau], wi_ref[tau]).astype(BF16)


def _ssm_kernel(u_ref, car_ref, cai_ref, wr_ref, wi_ref, pwr_ref, pwi_ref, d_ref, y_ref,
                s_ref, yin_ref, t_ref, win_ref, voutt_ref):
    nb = u_ref.shape[0] // KB
    ts = TILE_STATE

    @pl.when(pl.program_id(1) == 0)
    def _():
        _ssm_build_operators(car_ref, cai_ref, wr_ref, wi_ref, t_ref, win_ref, voutt_ref)

    u_cat = jnp.concatenate([u_ref[pl.ds(j, nb, stride=KB), :] for j in range(KB)], axis=1)
    u_bf = u_cat.astype(BF16)
    s_ref[...] = jnp.dot(u_bf, win_ref[...], preferred_element_type=F32)
    width = 2 * LANES
    chunks = [slice(c * width, (c + 1) * width) for c in range(KB * LANES // width)]
    row = lax.broadcasted_iota(jnp.int32, (SUBLANES, ts), 0)
    pw_re = pwr_ref[...]
    pw_im = pwi_ref[...]

    def group(r, carry):
        c_re, c_im = carry
        rows = pl.ds(r * SUBLANES, SUBLANES)
        x_re = s_ref[rows, 0:ts]
        x_im = s_ref[rows, ts:2 * ts]
        for k in (1, 2, 4):
            a_re = pw_re[k - 1:k]
            a_im = pw_im[k - 1:k]
            sh_re = _shift_rows(x_re, k, row)
            sh_im = _shift_rows(x_im, k, row)
            x_re, x_im = x_re + a_re * sh_re - a_im * sh_im, x_im + a_re * sh_im + a_im * sh_re
        x_re, x_im = x_re + pw_re * c_re - pw_im * c_im, x_im + pw_re * c_im + pw_im * c_re
        s_ref[rows, 0:ts] = jnp.where(row >= 1, pltpu.roll(x_re, 1, 0), c_re)
        s_ref[rows, ts:2 * ts] = jnp.where(row >= 1, pltpu.roll(x_im, 1, 0), c_im)
        return x_re[SUBLANES - 1:SUBLANES], x_im[SUBLANES - 1:SUBLANES]

    groups = nb // SUBLANES
    carry = (jnp.zeros((1, ts), F32),) * 2
    for r in range(groups):
        for c, cols in enumerate(chunks):
            if c * groups // len(chunks) == r:
                yin_ref[:, cols] = jnp.dot(u_bf[:, :cols.stop], t_ref[0:cols.stop, cols],
                                           preferred_element_type=F32)
        carry = group(r, carry)
    s_bf = s_ref[...].astype(BF16)
    for c, cols in enumerate(chunks):
        y = yin_ref[:, cols] + lax.dot_general(s_bf, voutt_ref[cols, :], NT_DIMS, preferred_element_type=F32)
        y = y + d_ref[:, cols] * u_cat[:, cols]
        for jj in range(width // LANES):
            y_ref[pl.ds(c * (width // LANES) + jj, nb, stride=KB), :] = y[:, jj * LANES:(jj + 1) * LANES]


def _ssm(u, car, cai, wr, wi, pwr, pwi, d_tiled, batch):
    seq = u.shape[0] // batch
    nb = seq // KB
    factor = pl.BlockSpec((KB + 1, S5_GROUP, TILE_STATE), lambda c, b: (0, 0, c))
    power = pl.BlockSpec((SUBLANES, TILE_STATE), lambda c, b: (0, c))
    io = pl.BlockSpec((seq, LANES), lambda c, b: (b, c))
    return pl.pallas_call(
        _ssm_kernel,
        grid=(N_TILES, batch),
        in_specs=[io, factor, factor, factor, factor, power, power,
                  pl.BlockSpec((None, 1, KB * LANES), lambda c, b: (c, 0, 0))],
        out_specs=io,
        out_shape=jax.ShapeDtypeStruct(u.shape, F32),
        scratch_shapes=[pltpu.VMEM((nb, 2 * TILE_STATE), F32),
                        pltpu.VMEM((nb, KB * LANES), F32),
                        pltpu.VMEM((KB * LANES, KB * LANES), BF16),
                        pltpu.VMEM((KB * LANES, 2 * TILE_STATE), BF16),
                        pltpu.VMEM((KB * LANES, 2 * TILE_STATE), BF16)],
        compiler_params=_params("parallel", "arbitrary"),
        name="s5_ssm",
    )(u, car, cai, wr, wi, pwr, pwi, d_tiled)


def _layer_norm(z, g, b):
    mu = jnp.mean(z, axis=-1, keepdims=True)
    zc = z - mu
    var = jnp.mean(zc * zc, axis=-1, keepdims=True)
    return zc * lax.rsqrt(var + LN_EPS) * g + b


def _tail_kernel(*refs, glu):
    if glu:
        (m_ref, x_ref, wglu_ref, bglu_ref, wout_ref, g1_ref, b1n_ref,
         w1_ref, b1_ref, w2_ref, b2_ref, g2_ref, b2n_ref, o_ref) = refs
    else:
        (m_ref, x_ref, wout_ref, g1_ref, b1n_ref,
         w1_ref, b1_ref, w2_ref, b2_ref, g2_ref, b2n_ref, o_ref) = refs
    tm = x_ref.shape[0]
    parts = [slice(h * tm // TAIL_SPLIT, (h + 1) * tm // TAIL_SPLIT) for h in range(TAIL_SPLIT)]
    mm = lambda a, b: jnp.dot(a, b, preferred_element_type=F32)
    if glu:
        g = [jax.nn.gelu(m_ref[rows, :]) for rows in parts]
        gate = [mm(gh.astype(BF16), wglu_ref[...]) + bglu_ref[...] for gh in g]
        m = [(gh * jax.nn.sigmoid(th)).astype(BF16) for gh, th in zip(g, gate)]
    else:
        m = [m_ref[rows, :] for rows in parts]
    mix = [mm(mh, wout_ref[...]) for mh in m]
    x1 = [_layer_norm(DEEPNORM_ALPHA * x_ref[rows, :] + mh, g1_ref[...], b1n_ref[...])
          for rows, mh in zip(parts, mix)]
    x1_bf = [a.astype(BF16) for a in x1]
    ff = [None] * TAIL_SPLIT
    for c in range(D_FF // D_MODEL):
        cols = slice(c * D_MODEL, (c + 1) * D_MODEL)
        hid = [jnp.square(jnp.maximum(mm(a, w1_ref[:, cols]) + b1_ref[:, cols], 0.0)) for a in x1_bf]
        for h in range(TAIL_SPLIT):
            out = mm(hid[h].astype(BF16), w2_ref[cols, :])
            ff[h] = out if c == 0 else ff[h] + out
    for h, rows in enumerate(parts):
        o_ref[rows, :] = _layer_norm(DEEPNORM_ALPHA * x1[h] + ff[h] + b2_ref[...], g2_ref[...], b2n_ref[...])


def _layer_tail(m, x, weights, glu):
    n = x.shape[0]
    tm = min(n, 512)
    row = pl.BlockSpec((tm, D_MODEL), lambda i: (i, 0))
    ops = [m, x]
    specs = [row, row]
    for wgt in weights:
        a = wgt.reshape(1, -1) if wgt.ndim == 1 else wgt
        ops.append(a)
        specs.append(_resident(a.shape))
    return pl.pallas_call(
        functools.partial(_tail_kernel, glu=glu),
        grid=(n // tm,),
        in_specs=specs,
        out_specs=row,
        out_shape=jax.ShapeDtypeStruct((n, D_MODEL), F32),
        compiler_params=_params("parallel"),
        name="layer_tail_glu" if glu else "layer_tail",
    )(*ops)


def _qkv_kernel(x_ref, w_ref, hsum_ref, q_ref, k_ref, v_ref, norm_ref):
    y = jnp.dot(x_ref[...].astype(BF16), w_ref[...], preferred_element_type=F32)
    q = (y[:, :D_MODEL] * (LOG2_E / math.sqrt(SB_HEAD_DIM))).astype(BF16)
    k = y[:, D_MODEL:2 * D_MODEL].astype(BF16)
    q_ref[...] = q
    k_ref[...] = k
    v_ref[...] = y[:, 2 * D_MODEL:].astype(BF16)

    def head_norm_max(a):
        tops = []
        for p in range(D_MODEL // LANES):
            sq = jnp.square(a[:, p * LANES:(p + 1) * LANES].astype(F32)).astype(BF16)
            tops.append(jnp.max(jnp.dot(sq, hsum_ref[...], preferred_element_type=F32), axis=0, keepdims=True))
        return jnp.concatenate(tops, axis=1)

    norm_ref[0:1, :] = head_norm_max(q)
    norm_ref[1:2, :] = head_norm_max(k)


def _qkv(x, w_qkv):
    n = x.shape[0]
    tm = min(n, QKV_ROWS)
    row = pl.BlockSpec((tm, D_MODEL), lambda i: (i, 0))
    out = jax.ShapeDtypeStruct((n, D_MODEL), BF16)
    hd = jnp.arange(LANES) // SB_HEAD_DIM
    hsum = (hd[:, None] == hd[None, :]).astype(BF16)
    return pl.pallas_call(
        _qkv_kernel,
        grid=(n // tm,),
        in_specs=[row, _resident(w_qkv.shape), _resident(hsum.shape)],
        out_specs=[row, row, row, pl.BlockSpec((None, 2, D_MODEL), lambda i: (i, 0, 0))],
        out_shape=[out, out, out, jax.ShapeDtypeStruct((n // tm, 2, D_MODEL), F32)],
        compiler_params=_params("parallel"),
        name="qkv_proj",
    )(x, w_qkv, hsum)


def _attn_kernel(q_ref, k_ref, v_ref, qn_ref, kn_ref, ntri_ref, ntrih_ref, o_ref, acc_ref, r_ref, st_ref):
    i = pl.program_id(2)
    tb = ATT_BLOCK
    pairs = q_ref.shape[1] // LANES
    lane = lax.broadcasted_iota(jnp.int32, (tb, LANES), 1)
    first = lane < SB_HEAD_DIM

    def split_heads(a):
        zero = jnp.zeros_like(a)
        mine = lax.broadcasted_iota(jnp.int32, a.shape, 1) < SB_HEAD_DIM
        return jnp.concatenate([jnp.where(mine, a, zero), jnp.where(mine, zero, a)], axis=0)

    st_ref[1] = jnp.sqrt(jnp.max(qn_ref[0:1, :]) * jnp.max(kn_ref[:, 1:2, :])) * 1.02 + 1e-3

    acc_ref[...] = jnp.zeros_like(acc_ref)
    r_ref[...] = jnp.zeros_like(r_ref)
    rr = lax.broadcasted_iota(jnp.int32, (tb, 2 * tb), 0)
    cc = lax.broadcasted_iota(jnp.int32, (tb, 2 * tb), 1)
    causal = jnp.bitwise_and(cc, tb - 1) < rr
    lanes = [slice(p * LANES, (p + 1) * LANES) for p in range(pairs)]

    hr, hk = HEAD_ROWS, HEAD_KEYS
    late = jnp.logical_not(jnp.logical_and(rr < hr, jnp.bitwise_and(cc, tb - 1) >= tb - hk))
    lane_h = lax.broadcasted_iota(jnp.int32, (hr, 2 * hk), 1)
    head_h = [lane_h < hk, lane_h >= hk]

    def cost_of(z, no_overflow):
        if no_overflow:
            return jnp.log2(1.0 + jnp.exp2(z))
        return jnp.maximum(z, 0.0) + jnp.log2(1.0 + 1.0 / jnp.exp2(jnp.abs(z)))

    def hi_lo(c):
        hi = c.astype(BF16)
        return jnp.concatenate([hi, (c - hi.astype(F32)).astype(BF16)], axis=1)

    def visit(blocks, masks, first=False, head_block=None):
        keys = [pl.ds(pl.multiple_of(j * tb, tb), tb) for j in blocks]
        todo = [(p, t) for p in range(pairs) for t in range(len(blocks))]
        z = {pt: lax.dot_general(q_ref[:, lanes[pt[0]]], split_heads(k_ref[keys[pt[1]], lanes[pt[0]]]), NT_DIMS,
                                 preferred_element_type=F32) for pt in todo}
        if head_block is not None:
            hkeys = pl.ds(pl.multiple_of(head_block * tb + (tb - hk), hk), hk)
            zh = [lax.dot_general(q_ref[0:hr, lanes[p]], split_heads(k_ref[hkeys, lanes[p]]), NT_DIMS,
                                  preferred_element_type=F32) for p in range(pairs)]

        def finish(no_overflow):
            cost = {}
            for pt in todo:
                c = cost_of(z[pt], no_overflow)
                cost[pt] = c if masks[pt[1]] is None else jnp.where(masks[pt[1]], c, 0.0)
            suffix = {pt: jnp.dot(hi_lo(cost[pt]), ntri_ref[...], preferred_element_type=F32) for pt in todo}
            if head_block is not None:
                cost_h = [cost_of(zh[p], no_overflow) for p in range(pairs)]
                suffix_h = [jnp.dot(hi_lo(cost_h[p]), ntrih_ref[...], preferred_element_type=F32)
                            for p in range(pairs)]
            r_top = None
            for p in range(pairs):
                r = r_ref[p]
                weights = []
                for t in range(len(blocks)):
                    w = jnp.exp2(z[p, t] + suffix[p, t] + r)
                    if masks[t] is not None:
                        w = jnp.where(masks[t], w, 0.0)
                    weights.append(w.astype(BF16))
                    spent = [jnp.broadcast_to(jnp.sum(cost[p, t][:, h * tb:(h + 1) * tb], axis=1, keepdims=True),
                                              (tb, tb)) for h in range(2)]
                    r = r - jnp.concatenate(spent, axis=1)
                values = jnp.concatenate([split_heads(v_ref[keys[t], lanes[p]]) for t in range(len(blocks))], axis=0)
                acc_ref[:, lanes[p]] += jnp.dot(jnp.concatenate(weights, axis=1), values,
                                                preferred_element_type=F32)
                if head_block is not None:
                    r_h = r[0:hr]
                    w = jnp.exp2(zh[p] + suffix_h[p] + jnp.where(head_h[0], r_h[:, 0:LANES], r_h[:, tb:tb + LANES]))
                    acc_ref[0:hr, lanes[p]] += jnp.dot(w.astype(BF16), split_heads(v_ref[hkeys, lanes[p]]),
                                                       preferred_element_type=F32)
                    spent = [jnp.broadcast_to(jnp.sum(jnp.where(head_h[h], cost_h[p], 0.0), axis=1, keepdims=True),
                                              (hr, tb)) for h in range(2)]
                    r = jnp.concatenate([r_h - jnp.concatenate(spent, axis=1), r[hr:]], axis=0)
                r_ref[p] = r
                r_top = r if r_top is None else jnp.maximum(r_top, r)
            st_ref[0] = jnp.max(r_top)

        if first and len(blocks) > 1:
            bounded = st_ref[1] < BOUNDED_SCORE
            pl.when(bounded)(functools.partial(finish, True))
            pl.when(jnp.logical_not(bounded))(functools.partial(finish, False))
        else:
            finish(False)
        return st_ref[0]

    def unfinished():
        return st_ref[0] + st_ref[1] > EXP2_FLOOR

    has_third = i >= 2

    @pl.when(has_third)
    def _():
        visit([i, i - 1], [causal, None], first=True, head_block=i - 2)

    @pl.when(jnp.logical_and(has_third, unfinished()))
    def _():
        visit([i - 2], [late])

    @pl.when(jnp.logical_not(has_third))
    def _():
        visit([i], [causal], first=True)

    def more(c):
        j, r_max = c
        return jnp.logical_and(j >= 0, r_max + st_ref[1] > EXP2_FLOOR)

    def step(c):
        j, _ = c
        return j - 1, visit([j], [None])

    lax.while_loop(more, step, (jnp.where(has_third, i - 3, i - 1), st_ref[0]))
    o_ref[...] = acc_ref[...].astype(o_ref.dtype)


def _attention(q, k, v, norms, batch):
    n = q.shape[0]
    seq = n // batch
    tile = n // norms.shape[0]
    tb = ATT_BLOCK
    nq = seq // tb
    width = ATT_PAIRS * LANES
    idx = jnp.arange(2 * tb)
    same_head = (idx[:, None] // tb) == (idx[None, :] // tb)
    ntri = -(same_head & (idx[:, None] >= idx[None, :])).astype(BF16)
    ntri = jnp.concatenate([ntri, ntri], axis=0)
    idh = jnp.arange(2 * HEAD_KEYS)
    ntrih = -((idh[:, None] // HEAD_KEYS == idh[None, :] // HEAD_KEYS) & (idh[:, None] >= idh[None, :])).astype(BF16)
    ntrih = jnp.concatenate([ntrih, ntrih], axis=0)
    qn_spec = pl.BlockSpec((None, 2, width), lambda b, h, i: ((b * nq + i) * tb // tile, 0, h))
    kn_spec = pl.BlockSpec((seq // tile, 2, width), lambda b, h, i: (b, 0, h))
    kv_spec = pl.BlockSpec((seq, width), lambda b, h, i: (b, h))
    q_spec = pl.BlockSpec((tb, width), lambda b, h, i: (b * nq + i, h))
    return pl.pallas_call(
        _attn_kernel,
        grid=(batch, D_MODEL // width, nq),
        in_specs=[q_spec, kv_spec, kv_spec, qn_spec, kn_spec, _resident(ntri.shape), _resident(ntrih.shape)],
        out_specs=q_spec,
        out_shape=jax.ShapeDtypeStruct((n, D_MODEL), BF16),
        scratch_shapes=[pltpu.VMEM((tb, width), F32), pltpu.VMEM((ATT_PAIRS, tb, 2 * tb), F32),
                        pltpu.SMEM((2,), F32)],
        compiler_params=_params("parallel", "parallel", "arbitrary"),
        name="sb_attention",
    )(q, k, v, norms, norms, ntri, ntrih)


def kernel(x, s5_w_in, s5_lambda_re, s5_lambda_im, s5_b_re, s5_b_im, s5_c_re, s5_c_im, s5_d, s5_log_step,
           s5_w_glu, s5_b_glu, s5_w_out, sb_w_kv, sb_w_q, sb_w_out, mlp_w1, mlp_b1, mlp_w2, mlp_b2,
           ln_mix_g, ln_mix_b, ln_mlp_g, ln_mlp_b):
    batch, seq, d = x.shape
    n = batch * seq
    bf = lambda a: a.astype(BF16)
    x0 = x.reshape(n, d)

    car, cai, wr, wi, pwr, pwi = _s5_prep(s5_lambda_re[0], s5_lambda_im[0], s5_log_step[0],
                                          s5_b_re[0], s5_b_im[0], s5_c_re[0], s5_c_im[0])
    d_tiled = jnp.tile(s5_d[0].reshape(N_TILES, 1, LANES), (1, 1, KB))
    u = _u_proj(x0, bf(s5_w_in[0]))
    y = _ssm(u, car, cai, wr, wi, pwr, pwi, d_tiled, batch)
    x1 = _layer_tail(
        y, x0,
        [bf(s5_w_glu[0]), s5_b_glu[0], bf(s5_w_out[0]), ln_mix_g[0], ln_mix_b[0],
         bf(mlp_w1[0]), mlp_b1[0], bf(mlp_w2[0]), mlp_b2[0], ln_mlp_g[0], ln_mlp_b[0]],
        glu=True)

    q, k, v, norms = _qkv(x1, bf(jnp.concatenate([sb_w_q[0], sb_w_kv], axis=1)))
    o = _attention(q, k, v, norms, batch)
    out = _layer_tail(
        o, x1,
        [bf(sb_w_out[0]), ln_mix_g[1], ln_mix_b[1],
         bf(mlp_w1[1]), mlp_b1[1], bf(mlp_w2[1]), mlp_b2[1], ln_mlp_g[1], ln_mlp_b[1]],
        glu=False)
    return out.reshape(batch, seq, d)
```

```python
import functools
import math

import jax
import jax.numpy as jnp
from jax import lax
from jax.experimental import pallas as pl
from jax.experimental.pallas import tpu as pltpu

F32 = jnp.float32
BF16 = jnp.bfloat16

D_MODEL = 1024
DEPTH = 2
S5_GROUP = 16
S5_GROUPS = D_MODEL // S5_GROUP
S5_STATE = 64
SB_HEADS = 16
SB_HEAD_DIM = D_MODEL // SB_HEADS
D_FF = 4 * D_MODEL
DEEPNORM_ALPHA = (2.0 * DEPTH) ** 0.25
LN_EPS = 1e-5

LANES = 128
SUBLANES = 8
KB = 16
GROUPS_PER_TILE = LANES // S5_GROUP
N_TILES = D_MODEL // LANES
TILE_STATE = GROUPS_PER_TILE * S5_STATE
ALL_STATE = S5_GROUPS * S5_STATE
VMEM_LIMIT = 56 * 1024 * 1024
TAIL_SPLIT = 2
ATT_BLOCK = 128
ATT_PAIRS = 8
QKV_ROWS = 512
HEAD_ROWS = 64
HEAD_KEYS = 64
EXP2_FLOOR = -130.0
BOUNDED_SCORE = 120.0
LOG2_E = 1.0 / math.log(2.0)
NT_DIMS = (((1,), (1,)), ((), ()))


def _params(*sem):
    return pltpu.CompilerParams(dimension_semantics=sem, vmem_limit_bytes=VMEM_LIMIT)


def _resident(shape):
    zeros = (0,) * len(shape)
    return pl.BlockSpec(shape, lambda *_: zeros, pipeline_mode=pl.Buffered(1))


def _cmul(ar, ai, br, bi):
    return ar * br - ai * bi, ar * bi + ai * br


def _s5_prep_kernel(lr_ref, li_ref, ls_ref, br_ref, bi_ref, cr_ref, ci_ref,
                    car_ref, cai_ref, wr_ref, wi_ref, pwr_ref, pwi_ref,
                    a1r_ref, a1i_ref, pr_ref, pi_ref, bbr_ref, bbi_ref):
    tau = pl.program_id(0)

    @pl.when(tau == 0)
    def _():
        lr = lr_ref[...]
        li = li_ref[...]
        dt = jnp.exp(ls_ref[...])
        mag = jnp.exp(lr * dt)
        ang = li * dt
        a_re = mag * jnp.cos(ang)
        a_im = mag * jnp.sin(ang)
        nr = a_re - 1.0
        ni = a_im
        den = lr * lr + li * li
        f_re = (nr * lr + ni * li) / den
        f_im = (ni * lr - nr * li) / den
        bbr, bbi = _cmul(f_re, f_im, br_ref[...], bi_ref[...])
        bbr_ref[...] = bbr
        bbi_ref[...] = bbi
        a1r_ref[...] = a_re
        a1i_ref[...] = a_im
        pr_ref[...] = jnp.ones_like(a_re)
        pi_ref[...] = jnp.zeros_like(a_im)

    p_re = pr_ref[...]
    p_im = pi_ref[...]
    car, cai = _cmul(cr_ref[...], ci_ref[...], p_re, p_im)
    car_ref[...] = car
    cai_ref[...] = -cai
    wr, wi = _cmul(p_re, p_im, bbr_ref[...], bbi_ref[...])
    wr_ref[...] = wr
    wi_ref[...] = wi

    @pl.when(tau == KB)
    def _():
        q_re, q_im = p_re, p_im
        for i in range(SUBLANES):
            pwr_ref[i:i + 1, :] = q_re
            pwi_ref[i:i + 1, :] = q_im
            q_re, q_im = _cmul(q_re, q_im, p_re, p_im)

    n_re, n_im = _cmul(p_re, p_im, a1r_ref[...], a1i_ref[...])
    pr_ref[...] = n_re
    pi_ref[...] = n_im


def _s5_prep(lam_re, lam_im, log_step, b_re, b_im, c_re, c_im):
    g, p, h = S5_GROUPS, S5_STATE, S5_GROUP
    lr = lam_re.reshape(1, ALL_STATE)
    li = lam_im.reshape(1, ALL_STATE)
    ls = jnp.broadcast_to(log_step[:, None], (g, p)).reshape(1, ALL_STATE)
    br = b_re.transpose(2, 0, 1).reshape(h, ALL_STATE)
    bi = b_im.transpose(2, 0, 1).reshape(h, ALL_STATE)
    cr = c_re.transpose(1, 0, 2).reshape(h, ALL_STATE)
    ci = c_im.transpose(1, 0, 2).reshape(h, ALL_STATE)
    small = pl.BlockSpec((1, ALL_STATE), lambda t: (0, 0))
    big = pl.BlockSpec((h, ALL_STATE), lambda t: (0, 0))
    step = pl.BlockSpec((None, h, ALL_STATE), lambda t: (t, 0, 0))
    pw = pl.BlockSpec((SUBLANES, ALL_STATE), lambda t: (0, 0))
    out3 = jax.ShapeDtypeStruct((KB + 1, h, ALL_STATE), F32)
    outp = jax.ShapeDtypeStruct((SUBLANES, ALL_STATE), F32)
    return pl.pallas_call(
        _s5_prep_kernel,
        grid=(KB + 1,),
        in_specs=[small, small, small, big, big, big, big],
        out_specs=[step, step, step, step, pw, pw],
        out_shape=[out3, out3, out3, out3, outp, outp],
        scratch_shapes=[pltpu.VMEM((1, ALL_STATE), F32)] * 4 + [pltpu.VMEM((h, ALL_STATE), F32)] * 2,
        compiler_params=_params("arbitrary"),
        name="s5_prep",
    )(lr, li, ls, br, bi, cr, ci)


def _uproj_kernel(x_ref, w_ref, o_ref):
    o_ref[...] = jnp.dot(x_ref[...].astype(BF16), w_ref[...], preferred_element_type=F32)


def _u_proj(x, w_in):
    n = x.shape[0]
    tm = min(n, 1024)
    row = pl.BlockSpec((tm, D_MODEL), lambda i: (i, 0))
    return pl.pallas_call(
        _uproj_kernel,
        grid=(n // tm,),
        in_specs=[row, _resident((D_MODEL, D_MODEL))],
        out_specs=row,
        out_shape=jax.ShapeDtypeStruct((n, D_MODEL), F32),
        compiler_params=_params("parallel"),
        name="s5_u_proj",
    )(x, w_in)


def _shift_rows(x, k, row):
    return jnp.where(row >= k, pltpu.roll(x, k, 0), 0.0)


def _ssm_build_operators(car_ref, cai_ref, wr_ref, wi_ref, t_ref, win_ref, voutt_ref):
    ts = TILE_STATE
    row_group = lax.broadcasted_iota(jnp.int32, (LANES, ts), 0) // S5_GROUP
    lane_group = lax.broadcasted_iota(jnp.int32, (LANES, ts), 1) // S5_STATE
    same_group = row_group == lane_group

    def expand(re16, im16):
        tile = lambda a: jnp.where(same_group, jnp.concatenate([a] * GROUPS_PER_TILE, axis=0), 0.0)
        return jnp.concatenate([tile(re16), tile(im16)], axis=1)

    t_ref[...] = jnp.zeros_like(t_ref)
    bbar = expand(wr_ref[0], wi_ref[0]).astype(BF16)
    for tau in range(KB + 1):
        ca = expand(car_ref[tau], cai_ref[tau]).astype(BF16)
        if tau >= 1:
            voutt_ref[(tau - 1) * LANES:tau * LANES, :] = ca
        if tau < KB:
            lag = lax.dot_general(bbar, ca, NT_DIMS, preferred_element_type=F32).astype(BF16)
            for j in range(KB - tau):
                t_ref[j * LANES:(j + 1) * LANES, (j + tau) * LANES:(j + tau + 1) * LANES] = lag
            win_ref[(KB - 1 - tau) * LANES:(KB - tau) * LANES, :] = expand(wr_ref[tau], wi_ref[tau]).astype(BF16)


def _ssm_kernel(u_ref, car_ref, cai_ref, wr_ref, wi_ref, pwr_ref, pwi_ref, d_ref, y_ref,
                s_ref, yin_ref, t_ref, win_ref, voutt_ref):
    nb = u_ref.shape[0] // KB
    ts = TILE_STATE

    @pl.when(pl.program_id(1) == 0)
    def _():
        _ssm_build_operators(car_ref, cai_ref, wr_ref, wi_ref, t_ref, win_ref, voutt_ref)

    u_cat = jnp.concatenate([u_ref[pl.ds(j, nb, stride=KB), :] for j in range(KB)], axis=1)
    u_bf = u_cat.astype(BF16)
    s_ref[...] = jnp.dot(u_bf, win_ref[...], preferred_element_type=F32)
    width = 2 * LANES
    chunks = [slice(c * width, (c + 1) * width) for c in range(KB * LANES // width)]
    row = lax.broadcasted_iota(jnp.int32, (SUBLANES, ts), 0)
    pw_re = pwr_ref[...]
    pw_im = pwi_ref[...]

    def group(r, carry):
        c_re, c_im = carry
        rows = pl.ds(r * SUBLANES, SUBLANES)
        x_re = s_ref[rows, 0:ts]
        x_im = s_ref[rows, ts:2 * ts]
        for k in (1, 2, 4):
            a_re = pw_re[k - 1:k]
            a_im = pw_im[k - 1:k]
            sh_re = _shift_rows(x_re, k, row)
            sh_im = _shift_rows(x_im, k, row)
            x_re, x_im = x_re + a_re * sh_re - a_im * sh_im, x_im + a_re * sh_im + a_im * sh_re
        x_re, x_im = x_re + pw_re * c_re - pw_im * c_im, x_im + pw_re * c_im + pw_im * c_re
        s_ref[rows, 0:ts] = jnp.where(row >= 1, pltpu.roll(x_re, 1, 0), c_re)
        s_ref[rows, ts:2 * ts] = jnp.where(row >= 1, pltpu.roll(x_im, 1, 0), c_im)
        return x_re[SUBLANES - 1:SUBLANES], x_im[SUBLANES - 1:SUBLANES]

    groups = nb // SUBLANES
    carry = (jnp.zeros((1, ts), F32),) * 2
    for r in range(groups):
        for c, cols in enumerate(chunks):
            if c * groups // len(chunks) == r:
                yin_ref[:, cols] = jnp.dot(u_bf[:, :cols.stop], t_ref[0:cols.stop, cols],
                                           preferred_element_type=F32)
        carry = group(r, carry)
    s_bf = s_ref[...].astype(BF16)
    for c, cols in enumerate(chunks):
        y = yin_ref[:, cols] + lax.dot_general(s_bf, voutt_ref[cols, :], NT_DIMS, preferred_element_type=F32)
        y = y + d_ref[:, cols] * u_cat[:, cols]
        for jj in range(width // LANES):
            y_ref[pl.ds(c * (width // LANES) + jj, nb, stride=KB), :] = y[:, jj * LANES:(jj + 1) * LANES]


def _ssm(u, car, cai, wr, wi, pwr, pwi, d_tiled, batch):
    seq = u.shape[0] // batch
    nb = seq // KB
    factor = pl.BlockSpec((KB + 1, S5_GROUP, TILE_STATE), lambda c, b: (0, 0, c))
    power = pl.BlockSpec((SUBLANES, TILE_STATE), lambda c, b: (0, c))
    io = pl.BlockSpec((seq, LANES), lambda c, b: (b, c))
    return pl.pallas_call(
        _ssm_kernel,
        grid=(N_TILES, batch),
        in_specs=[io, factor, factor, factor, factor, power, power,
                  pl.BlockSpec((None, 1, KB * LANES), lambda c, b: (c, 0, 0))],
        out_specs=io,
        out_shape=jax.ShapeDtypeStruct(u.shape, F32),
        scratch_shapes=[pltpu.VMEM((nb, 2 * TILE_STATE), F32),
                        pltpu.VMEM((nb, KB * LANES), F32),
                        pltpu.VMEM((KB * LANES, KB * LANES), BF16),
                        pltpu.VMEM((KB * LANES, 2 * TILE_STATE), BF16),
                        pltpu.VMEM((KB * LANES, 2 * TILE_STATE), BF16)],
        compiler_params=_params("parallel", "arbitrary"),
        name="s5_ssm",
    )(u, car, cai, wr, wi, pwr, pwi, d_tiled)


def _layer_norm(z, g, b):
    mu = jnp.mean(z, axis=-1, keepdims=True)
    zc = z - mu
    var = jnp.mean(zc * zc, axis=-1, keepdims=True)
    return zc * lax.rsqrt(var + LN_EPS) * g + b


def _tail_kernel(*refs, glu):
    if glu:
        (m_ref, x_ref, wglu_ref, bglu_ref, wout_ref, g1_ref, b1n_ref,
         w1_ref, b1_ref, w2_ref, b2_ref, g2_ref, b2n_ref, o_ref) = refs
    else:
        (m_ref, x_ref, wout_ref, g1_ref, b1n_ref,
         w1_ref, b1_ref, w2_ref, b2_ref, g2_ref, b2n_ref, o_ref) = refs
    tm = x_ref.shape[0]
    parts = [slice(h * tm // TAIL_SPLIT, (h + 1) * tm // TAIL_SPLIT) for h in range(TAIL_SPLIT)]
    mm = lambda a, b: jnp.dot(a, b, preferred_element_type=F32)
    if glu:
        g = [jax.nn.gelu(m_ref[rows, :]) for rows in parts]
        gate = [mm(gh.astype(BF16), wglu_ref[...]) + bglu_ref[...] for gh in g]
        m = [(gh * jax.nn.sigmoid(th)).astype(BF16) for gh, th in zip(g, gate)]
    else:
        m = [m_ref[rows, :] for rows in parts]
    mix = [mm(mh, wout_ref[...]) for mh in m]
    x1 = [_layer_norm(DEEPNORM_ALPHA * x_ref[rows, :] + mh, g1_ref[...], b1n_ref[...])
          for rows, mh in zip(parts, mix)]
    x1_bf = [a.astype(BF16) for a in x1]
    ff = [None] * TAIL_SPLIT
    for c in range(D_FF // D_MODEL):
        cols = slice(c * D_MODEL, (c + 1) * D_MODEL)
        hid = [jnp.square(jnp.maximum(mm(a, w1_ref[:, cols]) + b1_ref[:, cols], 0.0)) for a in x1_bf]
        for h in range(TAIL_SPLIT):
            out = mm(hid[h].astype(BF16), w2_ref[cols, :])
            ff[h] = out if c == 0 else ff[h] + out
    for h, rows in enumerate(parts):
        o_ref[rows, :] = _layer_norm(DEEPNORM_ALPHA * x1[h] + ff[h] + b2_ref[...], g2_ref[...], b2n_ref[...])


def _layer_tail(m, x, weights, glu):
    n = x.shape[0]
    tm = min(n, 512)
    row = pl.BlockSpec((tm, D_MODEL), lambda i: (i, 0))
    ops = [m, x]
    specs = [row, row]
    for wgt in weights:
        a = wgt.reshape(1, -1) if wgt.ndim == 1 else wgt
        ops.append(a)
        specs.append(_resident(a.shape))
    return pl.pallas_call(
        functools.partial(_tail_kernel, glu=glu),
        grid=(n // tm,),
        in_specs=specs,
        out_specs=row,
        out_shape=jax.ShapeDtypeStruct((n, D_MODEL), F32),
        compiler_params=_params("parallel"),
        name="layer_tail_glu" if glu else "layer_tail",
    )(*ops)


def _qkv_kernel(x_ref, w_ref, hsum_ref, q_ref, k_ref, v_ref, norm_ref):
    x = x_ref[...].astype(BF16)
    y = jnp.dot(x, w_ref[:, :2 * D_MODEL], preferred_element_type=F32)
    q = (y[:, :D_MODEL] * (LOG2_E / math.sqrt(SB_HEAD_DIM))).astype(BF16)
    k = y[:, D_MODEL:].astype(BF16)
    q_ref[...] = q
    k_ref[...] = k
    v_ref[...] = jnp.dot(x, w_ref[:, 2 * D_MODEL:], preferred_element_type=F32).astype(BF16)

    def head_norm_max(a):
        sq = jnp.square(a.astype(F32)).astype(BF16)
        return jnp.max(jnp.dot(sq, hsum_ref[...], preferred_element_type=F32), axis=0, keepdims=True)

    norm_ref[0:1, :] = head_norm_max(q)
    norm_ref[1:2, :] = head_norm_max(k)


def _qkv(x, w_qkv):
    n = x.shape[0]
    tm = min(n, QKV_ROWS)
    row = pl.BlockSpec((tm, D_MODEL), lambda i: (i, 0))
    out = jax.ShapeDtypeStruct((n, D_MODEL), BF16)
    hsum = (jnp.arange(D_MODEL)[:, None] // SB_HEAD_DIM == jnp.arange(LANES)[None, :]).astype(BF16)
    return pl.pallas_call(
        _qkv_kernel,
        grid=(n // tm,),
        in_specs=[row, _resident(w_qkv.shape), _resident(hsum.shape)],
        out_specs=[row, row, row, pl.BlockSpec((None, 2, LANES), lambda i: (i, 0, 0))],
        out_shape=[out, out, out, jax.ShapeDtypeStruct((n // tm, 2, LANES), F32)],
        compiler_params=_params("parallel"),
        name="qkv_proj",
    )(x, w_qkv, hsum)


def _attn_kernel(q_ref, k_ref, v_ref, qn_ref, kn_ref, ntri_ref, ntrih_ref, o_ref, acc_ref, r_ref, st_ref):
    i = pl.program_id(2)
    tb = ATT_BLOCK
    pairs = q_ref.shape[1] // LANES
    lane = lax.broadcasted_iota(jnp.int32, (tb, LANES), 1)
    first = lane < SB_HEAD_DIM

    def split_heads(a):
        zero = jnp.zeros_like(a)
        mine = lax.broadcasted_iota(jnp.int32, a.shape, 1) < SB_HEAD_DIM
        return jnp.concatenate([jnp.where(mine, a, zero), jnp.where(mine, zero, a)], axis=0)

    st_ref[1] = jnp.sqrt(jnp.max(qn_ref[0:1, :]) * jnp.max(kn_ref[:, 1:2, :])) * 1.02 + 1e-3

    rr = lax.broadcasted_iota(jnp.int32, (tb, 2 * tb), 0)
    cc = lax.broadcasted_iota(jnp.int32, (tb, 2 * tb), 1)
    causal = jnp.bitwise_and(cc, tb - 1) < rr
    lanes = [slice(p * LANES, (p + 1) * LANES) for p in range(pairs)]

    hr, hk = HEAD_ROWS, HEAD_KEYS
    late = jnp.logical_not(jnp.logical_and(rr < hr, jnp.bitwise_and(cc, tb - 1) >= tb - hk))
    lane_h = lax.broadcasted_iota(jnp.int32, (hr, 2 * hk), 1)
    head_h = [lane_h < hk, lane_h >= hk]

    def cost_of(z, no_overflow):
        if no_overflow:
            return jnp.log2(1.0 + jnp.exp2(z))
        return jnp.maximum(z, 0.0) + jnp.log2(1.0 + 1.0 / jnp.exp2(jnp.abs(z)))

    def hi_lo(c):
        hi = c.astype(BF16)
        return jnp.concatenate([hi, (c - hi.astype(F32)).astype(BF16)], axis=1)

    def visit(blocks, masks, first=False, head_block=None):
        keys = [pl.ds(pl.multiple_of(j * tb, tb), tb) for j in blocks]
        todo = [(p, t) for p in range(pairs) for t in range(len(blocks))]
        z = {pt: lax.dot_general(q_ref[:, lanes[pt[0]]], split_heads(k_ref[keys[pt[1]], lanes[pt[0]]]), NT_DIMS,
                                 preferred_element_type=F32) for pt in todo}
        if head_block is not None:
            hkeys = pl.ds(pl.multiple_of(head_block * tb + (tb - hk), hk), hk)
            zh = [lax.dot_general(q_ref[0:hr, lanes[p]], split_heads(k_ref[hkeys, lanes[p]]), NT_DIMS,
                                  preferred_element_type=F32) for p in range(pairs)]

        def finish(no_overflow):
            cost = {}
            for pt in todo:
                c = cost_of(z[pt], no_overflow)
                cost[pt] = c if masks[pt[1]] is None else jnp.where(masks[pt[1]], c, 0.0)
            suffix = {pt: jnp.dot(hi_lo(cost[pt]), ntri_ref[...], preferred_element_type=F32) for pt in todo}
            if head_block is not None:
                cost_h = [cost_of(zh[p], no_overflow) for p in range(pairs)]
                suffix_h = [jnp.dot(hi_lo(cost_h[p]), ntrih_ref[...], preferred_element_type=F32)
                            for p in range(pairs)]
            r_top = None
            for p in range(pairs):
                r = jnp.zeros((tb, 2 * tb), F32) if first else r_ref[p]
                weights = []
                for t in range(len(blocks)):
                    w = jnp.exp2(z[p, t] + suffix[p, t] + r)
                    if masks[t] is not None:
                        w = jnp.where(masks[t], w, 0.0)
                    weights.append(w.astype(BF16))
                    spent = [jnp.broadcast_to(jnp.sum(cost[p, t][:, h * tb:(h + 1) * tb], axis=1, keepdims=True),
                                              (tb, tb)) for h in range(2)]
                    r = r - jnp.concatenate(spent, axis=1)
                values = jnp.concatenate([split_heads(v_ref[keys[t], lanes[p]]) for t in range(len(blocks))], axis=0)
                out = jnp.dot(jnp.concatenate(weights, axis=1), values, preferred_element_type=F32)
                acc_ref[:, lanes[p]] = out if first else acc_ref[:, lanes[p]] + out
                if head_block is not None:
                    r_h = r[0:hr]
                    w = jnp.exp2(zh[p] + suffix_h[p] + jnp.where(head_h[0], r_h[:, 0:LANES], r_h[:, tb:tb + LANES]))
                    acc_ref[0:hr, lanes[p]] += jnp.dot(w.astype(BF16), split_heads(v_ref[hkeys, lanes[p]]),
                                                       preferred_element_type=F32)
                    spent = [jnp.broadcast_to(jnp.sum(jnp.where(head_h[h], cost_h[p], 0.0), axis=1, keepdims=True),
                                              (hr, tb)) for h in range(2)]
                    r = jnp.concatenate([r_h - jnp.concatenate(spent, axis=1), r[hr:]], axis=0)
                r_ref[p] = r
                r_top = r if r_top is None else jnp.maximum(r_top, r)
            st_ref[0] = jnp.max(r_top)

        if first and len(blocks) > 1:
            bounded = st_ref[1] < BOUNDED_SCORE
            pl.when(bounded)(functools.partial(finish, True))
            pl.when(jnp.logical_not(bounded))(functools.partial(finish, False))
        else:
            finish(False)
        return st_ref[0]

    def unfinished():
        return st_ref[0] + st_ref[1] > EXP2_FLOOR

    has_third = i >= 2

    @pl.when(has_third)
    def _():
        visit([i, i - 1], [causal, None], first=True, head_block=i - 2)

    @pl.when(jnp.logical_and(has_third, unfinished()))
    def _():
        visit([i - 2], [late])

    @pl.when(jnp.logical_not(has_third))
    def _():
        visit([i], [causal], first=True)

    def more(c):
        j, r_max = c
        return jnp.logical_and(j >= 0, r_max + st_ref[1] > EXP2_FLOOR)

    def step(c):
        j, _ = c
        return j - 1, visit([j], [None])

    lax.while_loop(more, step, (jnp.where(has_third, i - 3, i - 1), st_ref[0]))
    o_ref[...] = acc_ref[...].astype(o_ref.dtype)


def _attention(q, k, v, norms, batch):
    n = q.shape[0]
    seq = n // batch
    tile = n // norms.shape[0]
    tb = ATT_BLOCK
    nq = seq // tb
    width = ATT_PAIRS * LANES
    idx = jnp.arange(2 * tb)
    same_head = (idx[:, None] // tb) == (idx[None, :] // tb)
    ntri = -(same_head & (idx[:, None] >= idx[None, :])).astype(BF16)
    ntri = jnp.concatenate([ntri, ntri], axis=0)
    idh = jnp.arange(2 * HEAD_KEYS)
    ntrih = -((idh[:, None] // HEAD_KEYS == idh[None, :] // HEAD_KEYS) & (idh[:, None] >= idh[None, :])).astype(BF16)
    ntrih = jnp.concatenate([ntrih, ntrih], axis=0)
    qn_spec = pl.BlockSpec((None, 2, LANES), lambda b, h, i: ((b * nq + i) * tb // tile, 0, 0))
    kn_spec = pl.BlockSpec((seq // tile, 2, LANES), lambda b, h, i: (b, 0, 0))
    kv_spec = pl.BlockSpec((seq, width), lambda b, h, i: (b, h))
    q_spec = pl.BlockSpec((tb, width), lambda b, h, i: (b * nq + i, h))
    return pl.pallas_call(
        _attn_kernel,
        grid=(batch, D_MODEL // width, nq),
        in_specs=[q_spec, kv_spec, kv_spec, qn_spec, kn_spec, _resident(ntri.shape), _resident(ntrih.shape)],
        out_specs=q_spec,
        out_shape=jax.ShapeDtypeStruct((n, D_MODEL), BF16),
        scratch_shapes=[pltpu.VMEM((tb, width), F32), pltpu.VMEM((ATT_PAIRS, tb, 2 * tb), F32),
                        pltpu.SMEM((2,), F32)],
        compiler_params=_params("parallel", "parallel", "arbitrary"),
        name="sb_attention",
    )(q, k, v, norms, norms, ntri, ntrih)


def kernel(x, s5_w_in, s5_lambda_re, s5_lambda_im, s5_b_re, s5_b_im, s5_c_re, s5_c_im, s5_d, s5_log_step,
           s5_w_glu, s5_b_glu, s5_w_out, sb_w_kv, sb_w_q, sb_w_out, mlp_w1, mlp_b1, mlp_w2, mlp_b2,
           ln_mix_g, ln_mix_b, ln_mlp_g, ln_mlp_b):
    batch, seq, d = x.shape
    n = batch * seq
    bf = lambda a: a.astype(BF16)
    x0 = x.reshape(n, d)

    car, cai, wr, wi, pwr, pwi = _s5_prep(s5_lambda_re[0], s5_lambda_im[0], s5_log_step[0],
                                          s5_b_re[0], s5_b_im[0], s5_c_re[0], s5_c_im[0])
    d_tiled = jnp.tile(s5_d[0].reshape(N_TILES, 1, LANES), (1, 1, KB))
    u = _u_proj(x0, bf(s5_w_in[0]))
    y = _ssm(u, car, cai, wr, wi, pwr, pwi, d_tiled, batch)
    x1 = _layer_tail(
        y, x0,
        [bf(s5_w_glu[0]), s5_b_glu[0], bf(s5_w_out[0]), ln_mix_g[0], ln_mix_b[0],
         bf(mlp_w1[0]), mlp_b1[0], bf(mlp_w2[0]), mlp_b2[0], ln_mlp_g[0], ln_mlp_b[0]],
        glu=True)

    q, k, v, norms = _qkv(x1, bf(jnp.concatenate([sb_w_q[0], sb_w_kv], axis=1)))
    o = _attention(q, k, v, norms, batch)
    out = _layer_tail(
        o, x1,
        [bf(sb_w_out[0]), ln_mix_g[1], ln_mix_b[1],
         bf(mlp_w1[1]), mlp_b1[1], bf(mlp_w2[1]), mlp_b2[1], ln_mlp_g[1], ln_mlp_b[1]],
        glu=False)
    return out.reshape(batch, seq, d)
```

```python
import functools
import math

import jax
import jax.numpy as jnp
import numpy as np
from jax import lax
from jax.experimental import pallas as pl
from jax.experimental.pallas import tpu as pltpu

F32 = jnp.float32
BF16 = jnp.bfloat16

D_MODEL = 1024
DEPTH = 2
S5_GROUP = 16
S5_GROUPS = D_MODEL // S5_GROUP
S5_STATE = 64
SB_HEADS = 16
SB_HEAD_DIM = D_MODEL // SB_HEADS
D_FF = 4 * D_MODEL
DEEPNORM_ALPHA = (2.0 * DEPTH) ** 0.25
LN_EPS = 1e-5

LANES = 128
SUBLANES = 8
KB = 16
GROUPS_PER_TILE = LANES // S5_GROUP
N_TILES = D_MODEL // LANES
TILE_STATE = GROUPS_PER_TILE * S5_STATE
ALL_STATE = S5_GROUPS * S5_STATE
VMEM_LIMIT = 56 * 1024 * 1024
TAIL_SPLIT = 2
ATT_BLOCK = 128
ATT_PAIRS = 8
QKV_ROWS = 512
HEAD_ROWS = 64
HEAD_KEYS = 64
EXP2_FLOOR = -130.0
BOUNDED_SCORE = 120.0
LOG2_E = 1.0 / math.log(2.0)
NT_DIMS = (((1,), (1,)), ((), ()))


def _params(*sem):
    return pltpu.CompilerParams(dimension_semantics=sem, vmem_limit_bytes=VMEM_LIMIT)


def _resident(shape, layer=None):
    zeros = (0,) * len(shape)
    if layer is None:
        return pl.BlockSpec(shape, lambda *_: zeros, pipeline_mode=pl.Buffered(1))
    return pl.BlockSpec((None,) + shape, lambda *_: (layer,) + zeros, pipeline_mode=pl.Buffered(1))


def _cmul(ar, ai, br, bi):
    return ar * br - ai * bi, ar * bi + ai * br


def _s5_prep_kernel(lr_ref, li_ref, ls_ref, br_ref, bi_ref, cr_ref, ci_ref,
                    car_ref, cai_ref, wr_ref, wi_ref, pwr_ref, pwi_ref,
                    a1r_ref, a1i_ref, pr_ref, pi_ref, bbr_ref, bbi_ref):
    tau = pl.program_id(0)

    @pl.when(tau == 0)
    def _():
        lr = lr_ref[...]
        li = li_ref[...]
        dt = jnp.exp(ls_ref[...])
        mag = jnp.exp(lr * dt)
        ang = li * dt
        a_re = mag * jnp.cos(ang)
        a_im = mag * jnp.sin(ang)
        nr = a_re - 1.0
        ni = a_im
        den = lr * lr + li * li
        f_re = (nr * lr + ni * li) / den
        f_im = (ni * lr - nr * li) / den
        bbr, bbi = _cmul(f_re, f_im, br_ref[...], bi_ref[...])
        bbr_ref[...] = bbr
        bbi_ref[...] = bbi
        a1r_ref[...] = a_re
        a1i_ref[...] = a_im
        pr_ref[...] = jnp.ones_like(a_re)
        pi_ref[...] = jnp.zeros_like(a_im)

    p_re = pr_ref[...]
    p_im = pi_ref[...]
    car, cai = _cmul(cr_ref[...], ci_ref[...], p_re, p_im)
    car_ref[...] = car
    cai_ref[...] = -cai
    wr, wi = _cmul(p_re, p_im, bbr_ref[...], bbi_ref[...])
    wr_ref[...] = wr
    wi_ref[...] = wi

    @pl.when(tau == KB)
    def _():
        q_re, q_im = p_re, p_im
        for i in range(SUBLANES):
            pwr_ref[i:i + 1, :] = q_re
            pwi_ref[i:i + 1, :] = q_im
            q_re, q_im = _cmul(q_re, q_im, p_re, p_im)

    n_re, n_im = _cmul(p_re, p_im, a1r_ref[...], a1i_ref[...])
    pr_ref[...] = n_re
    pi_ref[...] = n_im


def _s5_prep(lam_re, lam_im, log_step, b_re, b_im, c_re, c_im):
    g, p, h = S5_GROUPS, S5_STATE, S5_GROUP
    lr = lam_re.reshape(1, ALL_STATE)
    li = lam_im.reshape(1, ALL_STATE)
    ls = jnp.broadcast_to(log_step[:, None], (g, p)).reshape(1, ALL_STATE)
    br = b_re.transpose(2, 0, 1).reshape(h, ALL_STATE)
    bi = b_im.transpose(2, 0, 1).reshape(h, ALL_STATE)
    cr = c_re.transpose(1, 0, 2).reshape(h, ALL_STATE)
    ci = c_im.transpose(1, 0, 2).reshape(h, ALL_STATE)
    small = pl.BlockSpec((1, ALL_STATE), lambda t: (0, 0))
    big = pl.BlockSpec((h, ALL_STATE), lambda t: (0, 0))
    step = pl.BlockSpec((None, h, ALL_STATE), lambda t: (t, 0, 0))
    pw = pl.BlockSpec((SUBLANES, ALL_STATE), lambda t: (0, 0))
    out3 = jax.ShapeDtypeStruct((KB + 1, h, ALL_STATE), F32)
    outp = jax.ShapeDtypeStruct((SUBLANES, ALL_STATE), F32)
    return pl.pallas_call(
        _s5_prep_kernel,
        grid=(KB + 1,),
        in_specs=[small, small, small, big, big, big, big],
        out_specs=[step, step, step, step, pw, pw],
        out_shape=[out3, out3, out3, out3, outp, outp],
        scratch_shapes=[pltpu.VMEM((1, ALL_STATE), F32)] * 4 + [pltpu.VMEM((h, ALL_STATE), F32)] * 2,
        compiler_params=_params("arbitrary"),
        name="s5_prep",
    )(lr, li, ls, br, bi, cr, ci)


def _uproj_kernel(x_ref, w_ref, o_ref):
    o_ref[...] = jnp.dot(x_ref[...].astype(BF16), w_ref[...], preferred_element_type=F32)


def _u_proj(x, w_in):
    n = x.shape[0]
    tm = min(n, 1024)
    row = pl.BlockSpec((tm, D_MODEL), lambda i: (i, 0))
    return pl.pallas_call(
        _uproj_kernel,
        grid=(n // tm,),
        in_specs=[row, _resident((D_MODEL, D_MODEL))],
        out_specs=row,
        out_shape=jax.ShapeDtypeStruct((n, D_MODEL), F32),
        compiler_params=_params("parallel"),
        name="s5_u_proj",
    )(x, w_in)


def _shift_rows(x, k, row):
    return jnp.where(row >= k, pltpu.roll(x, k, 0), 0.0)


def _ssm_build_operators(car_ref, cai_ref, wr_ref, wi_ref, t_ref, win_ref, voutt_ref):
    ts = TILE_STATE
    row_group = lax.broadcasted_iota(jnp.int32, (LANES, ts), 0) // S5_GROUP
    lane_group = lax.broadcasted_iota(jnp.int32, (LANES, ts), 1) // S5_STATE
    same_group = row_group == lane_group

    def expand(re16, im16):
        tile = lambda a: jnp.where(same_group, jnp.concatenate([a] * GROUPS_PER_TILE, axis=0), 0.0)
        return jnp.concatenate([tile(re16), tile(im16)], axis=1)

    t_ref[...] = jnp.zeros_like(t_ref)
    bbar = expand(wr_ref[0], wi_ref[0]).astype(BF16)
    for tau in range(KB + 1):
        ca = expand(car_ref[tau], cai_ref[tau]).astype(BF16)
        if tau >= 1:
            voutt_ref[(tau - 1) * LANES:tau * LANES, :] = ca
        if tau < KB:
            lag = lax.dot_general(bbar, ca, NT_DIMS, preferred_element_type=F32).astype(BF16)
            for j in range(KB - tau):
                t_ref[j * LANES:(j + 1) * LANES, (j + tau) * LANES:(j + tau + 1) * LANES] = lag
            win_ref[(KB - 1 - tau) * LANES:(KB - tau) * LANES, :] = expand(wr_ref[tau], wi_ref[tau]).astype(BF16)


def _ssm_kernel(u_ref, car_ref, cai_ref, wr_ref, wi_ref, pwr_ref, pwi_ref, d_ref, y_ref,
                s_ref, yin_ref, t_ref, win_ref, voutt_ref):
    nb = u_ref.shape[0] // KB
    ts = TILE_STATE

    @pl.when(pl.program_id(1) == 0)
    def _():
        _ssm_build_operators(car_ref, cai_ref, wr_ref, wi_ref, t_ref, win_ref, voutt_ref)

    u_cat = jnp.concatenate([u_ref[pl.ds(j, nb, stride=KB), :] for j in range(KB)], axis=1)
    u_bf = u_cat.astype(BF16)
    s_ref[...] = jnp.dot(u_bf, win_ref[...], preferred_element_type=F32)
    width = 2 * LANES
    chunks = [slice(c * width, (c + 1) * width) for c in range(KB * LANES // width)]
    row = lax.broadcasted_iota(jnp.int32, (SUBLANES, ts), 0)
    pw_re = pwr_ref[...]
    pw_im = pwi_ref[...]

    def group(r, carry):
        c_re, c_im = carry
        rows = pl.ds(r * SUBLANES, SUBLANES)
        x_re = s_ref[rows, 0:ts]
        x_im = s_ref[rows, ts:2 * ts]
        for k in (1, 2, 4):
            a_re = pw_re[k - 1:k]
            a_im = pw_im[k - 1:k]
            sh_re = _shift_rows(x_re, k, row)
            sh_im = _shift_rows(x_im, k, row)
            x_re, x_im = x_re + a_re * sh_re - a_im * sh_im, x_im + a_re * sh_im + a_im * sh_re
        x_re, x_im = x_re + pw_re * c_re - pw_im * c_im, x_im + pw_re * c_im + pw_im * c_re
        s_ref[rows, 0:ts] = jnp.where(row >= 1, pltpu.roll(x_re, 1, 0), c_re)
        s_ref[rows, ts:2 * ts] = jnp.where(row >= 1, pltpu.roll(x_im, 1, 0), c_im)
        return x_re[SUBLANES - 1:SUBLANES], x_im[SUBLANES - 1:SUBLANES]

    groups = nb // SUBLANES
    carry = (jnp.zeros((1, ts), F32),) * 2
    for r in range(groups):
        for c, cols in enumerate(chunks):
            if c * groups // len(chunks) == r:
                yin_ref[:, cols] = jnp.dot(u_bf[:, :cols.stop], t_ref[0:cols.stop, cols],
                                           preferred_element_type=F32)
        carry = group(r, carry)
    s_bf = s_ref[...].astype(BF16)
    for c, cols in enumerate(chunks):
        y = yin_ref[:, cols] + lax.dot_general(s_bf, voutt_ref[cols, :], NT_DIMS, preferred_element_type=F32)
        y = y + d_ref[:, cols] * u_cat[:, cols]
        for jj in range(width // LANES):
            y_ref[pl.ds(c * (width // LANES) + jj, nb, stride=KB), :] = y[:, jj * LANES:(jj + 1) * LANES]


def _ssm(u, car, cai, wr, wi, pwr, pwi, d_tiled, batch):
    seq = u.shape[0] // batch
    nb = seq // KB
    factor = pl.BlockSpec((KB + 1, S5_GROUP, TILE_STATE), lambda c, b: (0, 0, c))
    power = pl.BlockSpec((SUBLANES, TILE_STATE), lambda c, b: (0, c))
    io = pl.BlockSpec((seq, LANES), lambda c, b: (b, c))
    return pl.pallas_call(
        _ssm_kernel,
        grid=(N_TILES, batch),
        in_specs=[io, factor, factor, factor, factor, power, power,
                  pl.BlockSpec((None, 1, KB * LANES), lambda c, b: (c, 0, 0))],
        out_specs=io,
        out_shape=jax.ShapeDtypeStruct(u.shape, F32),
        scratch_shapes=[pltpu.VMEM((nb, 2 * TILE_STATE), F32),
                        pltpu.VMEM((nb, KB * LANES), F32),
                        pltpu.VMEM((KB * LANES, KB * LANES), BF16),
                        pltpu.VMEM((KB * LANES, 2 * TILE_STATE), BF16),
                        pltpu.VMEM((KB * LANES, 2 * TILE_STATE), BF16)],
        compiler_params=_params("parallel", "arbitrary"),
        name="s5_ssm",
    )(u, car, cai, wr, wi, pwr, pwi, d_tiled)


def _layer_norm(z, g, b):
    mu = jnp.mean(z, axis=-1, keepdims=True)
    zc = z - mu
    var = jnp.mean(zc * zc, axis=-1, keepdims=True)
    return zc * lax.rsqrt(var + LN_EPS) * g + b


def _tail_kernel(*refs, glu):
    if glu:
        (m_ref, x_ref, wglu_ref, bglu_ref, wout_ref, g1_ref, b1n_ref,
         w1_ref, b1_ref, w2_ref, b2_ref, g2_ref, b2n_ref, o_ref) = refs
    else:
        (m_ref, x_ref, wout_ref, g1_ref, b1n_ref,
         w1_ref, b1_ref, w2_ref, b2_ref, g2_ref, b2n_ref, o_ref) = refs
    tm = x_ref.shape[0]
    parts = [slice(h * tm // TAIL_SPLIT, (h + 1) * tm // TAIL_SPLIT) for h in range(TAIL_SPLIT)]
    mm = lambda a, b: jnp.dot(a, b, preferred_element_type=F32)
    if glu:
        g = [jax.nn.gelu(m_ref[rows, :]) for rows in parts]
        gate = [mm(gh.astype(BF16), wglu_ref[...]) + bglu_ref[...] for gh in g]
        m = [(gh * jax.nn.sigmoid(th)).astype(BF16) for gh, th in zip(g, gate)]
    else:
        m = [m_ref[rows, :] for rows in parts]
    mix = [mm(mh, wout_ref[...]) for mh in m]
    x1 = [_layer_norm(DEEPNORM_ALPHA * x_ref[rows, :] + mh, g1_ref[...], b1n_ref[...])
          for rows, mh in zip(parts, mix)]
    x1_bf = [a.astype(BF16) for a in x1]
    ff = [None] * TAIL_SPLIT
    for c in range(D_FF // D_MODEL):
        cols = slice(c * D_MODEL, (c + 1) * D_MODEL)
        hid = [jnp.square(jnp.maximum(mm(a, w1_ref[:, cols]) + b1_ref[:, cols], 0.0)) for a in x1_bf]
        for h in range(TAIL_SPLIT):
            out = mm(hid[h].astype(BF16), w2_ref[cols, :])
            ff[h] = out if c == 0 else ff[h] + out
    for h, rows in enumerate(parts):
        o_ref[rows, :] = _layer_norm(DEEPNORM_ALPHA * x1[h] + ff[h] + b2_ref[...], g2_ref[...], b2n_ref[...])


def _layer_tail(m, x, weights, glu):
    n = x.shape[0]
    tm = min(n, 512)
    row = pl.BlockSpec((tm, D_MODEL), lambda i: (i, 0))
    ops = [m, x]
    specs = [row, row]
    for wgt in weights:
        if isinstance(wgt, tuple):
            ops.append(wgt[0])
            specs.append(_resident(wgt[0].shape[1:], layer=wgt[1]))
        else:
            a = wgt.reshape(1, -1) if wgt.ndim == 1 else wgt
            ops.append(a)
            specs.append(_resident(a.shape))
    return pl.pallas_call(
        functools.partial(_tail_kernel, glu=glu),
        grid=(n // tm,),
        in_specs=specs,
        out_specs=row,
        out_shape=jax.ShapeDtypeStruct((n, D_MODEL), F32),
        compiler_params=_params("parallel"),
        name="layer_tail_glu" if glu else "layer_tail",
    )(*ops)


def _qkv_kernel(x_ref, w_ref, hsum_ref, q_ref, k_ref, v_ref, norm_ref):
    x = x_ref[...].astype(BF16)
    y = jnp.dot(x, w_ref[:, :2 * D_MODEL], preferred_element_type=F32)
    q = (y[:, :D_MODEL] * (LOG2_E / math.sqrt(SB_HEAD_DIM))).astype(BF16)
    k = y[:, D_MODEL:].astype(BF16)
    q_ref[...] = q
    k_ref[...] = k
    v_ref[...] = jnp.dot(x, w_ref[:, 2 * D_MODEL:], preferred_element_type=F32).astype(BF16)

    def head_norm_max(a):
        sq = jnp.square(a.astype(F32)).astype(BF16)
        return jnp.max(jnp.dot(sq, hsum_ref[...], preferred_element_type=F32), axis=0, keepdims=True)

    norm_ref[0:1, :] = head_norm_max(q)
    norm_ref[1:2, :] = head_norm_max(k)


def _qkv(x, w_qkv):
    n = x.shape[0]
    tm = min(n, QKV_ROWS)
    row = pl.BlockSpec((tm, D_MODEL), lambda i: (i, 0))
    out = jax.ShapeDtypeStruct((n, D_MODEL), BF16)
    hsum = (np.arange(D_MODEL)[:, None] // SB_HEAD_DIM == np.arange(LANES)[None, :]).astype(BF16)
    return pl.pallas_call(
        _qkv_kernel,
        grid=(n // tm,),
        in_specs=[row, _resident(w_qkv.shape), _resident(hsum.shape)],
        out_specs=[row, row, row, pl.BlockSpec((None, 2, LANES), lambda i: (i, 0, 0))],
        out_shape=[out, out, out, jax.ShapeDtypeStruct((n // tm, 2, LANES), F32)],
        compiler_params=_params("parallel"),
        name="qkv_proj",
    )(x, w_qkv, hsum)


def _attn_kernel(q_ref, k_ref, v_ref, qn_ref, kn_ref, ntri_ref, ntrih_ref, o_ref, acc_ref, r_ref, st_ref):
    i = pl.program_id(2)
    tb = ATT_BLOCK
    pairs = q_ref.shape[1] // LANES
    lane = lax.broadcasted_iota(jnp.int32, (tb, LANES), 1)
    first = lane < SB_HEAD_DIM

    def split_heads(a):
        zero = jnp.zeros_like(a)
        mine = lax.broadcasted_iota(jnp.int32, a.shape, 1) < SB_HEAD_DIM
        return jnp.concatenate([jnp.where(mine, a, zero), jnp.where(mine, zero, a)], axis=0)

    st_ref[1] = jnp.sqrt(jnp.max(qn_ref[0:1, :]) * jnp.max(kn_ref[:, 1:2, :])) * 1.02 + 1e-3

    rr = lax.broadcasted_iota(jnp.int32, (tb, 2 * tb), 0)
    cc = lax.broadcasted_iota(jnp.int32, (tb, 2 * tb), 1)
    causal = jnp.bitwise_and(cc, tb - 1) < rr
    lanes = [slice(p * LANES, (p + 1) * LANES) for p in range(pairs)]

    hr, hk = HEAD_ROWS, HEAD_KEYS
    late = jnp.logical_not(jnp.logical_and(rr < hr, jnp.bitwise_and(cc, tb - 1) >= tb - hk))
    lane_h = lax.broadcasted_iota(jnp.int32, (hr, 2 * hk), 1)
    head_h = [lane_h < hk, lane_h >= hk]

    def cost_of(z, no_overflow):
        if no_overflow:
            return jnp.log2(1.0 + jnp.exp2(z))
        return jnp.maximum(z, 0.0) + jnp.log2(1.0 + 1.0 / jnp.exp2(jnp.abs(z)))

    def hi_lo(c):
        hi = c.astype(BF16)
        return jnp.concatenate([hi, (c - hi.astype(F32)).astype(BF16)], axis=1)

    def visit(blocks, masks, first=False, head_block=None):
        keys = [pl.ds(pl.multiple_of(j * tb, tb), tb) for j in blocks]
        todo = [(p, t) for p in range(pairs) for t in range(len(blocks))]
        z = {pt: lax.dot_general(q_ref[:, lanes[pt[0]]], split_heads(k_ref[keys[pt[1]], lanes[pt[0]]]), NT_DIMS,
                                 preferred_element_type=F32) for pt in todo}
        if head_block is not None:
            hkeys = pl.ds(pl.multiple_of(head_block * tb + (tb - hk), hk), hk)
            zh = [lax.dot_general(q_ref[0:hr, lanes[p]], split_heads(k_ref[hkeys, lanes[p]]), NT_DIMS,
                                  preferred_element_type=F32) for p in range(pairs)]

        def finish(no_overflow):
            cost = {}
            for pt in todo:
                c = cost_of(z[pt], no_overflow)
                cost[pt] = c if masks[pt[1]] is None else jnp.where(masks[pt[1]], c, 0.0)
            suffix = {pt: jnp.dot(hi_lo(cost[pt]), ntri_ref[...], preferred_element_type=F32) for pt in todo}
            if head_block is not None:
                cost_h = [cost_of(zh[p], no_overflow) for p in range(pairs)]
                suffix_h = [jnp.dot(hi_lo(cost_h[p]), ntrih_ref[...], preferred_element_type=F32)
                            for p in range(pairs)]
            r_top = None
            for p in range(pairs):
                r = jnp.zeros((tb, 2 * tb), F32) if first else r_ref[p]
                weights = []
                for t in range(len(blocks)):
                    w = jnp.exp2(z[p, t] + suffix[p, t] + r)
                    if masks[t] is not None:
                        w = jnp.where(masks[t], w, 0.0)
                    weights.append(w.astype(BF16))
                    spent = [jnp.broadcast_to(jnp.sum(cost[p, t][:, h * tb:(h + 1) * tb], axis=1, keepdims=True),
                                              (tb, tb)) for h in range(2)]
                    r = r - jnp.concatenate(spent, axis=1)
                values = jnp.concatenate([split_heads(v_ref[keys[t], lanes[p]]) for t in range(len(blocks))], axis=0)
                out = jnp.dot(jnp.concatenate(weights, axis=1), values, preferred_element_type=F32)
                acc_ref[:, lanes[p]] = out if first else acc_ref[:, lanes[p]] + out
                if head_block is not None:
                    r_h = r[0:hr]
                    w = jnp.exp2(zh[p] + suffix_h[p] + jnp.where(head_h[0], r_h[:, 0:LANES], r_h[:, tb:tb + LANES]))
                    acc_ref[0:hr, lanes[p]] += jnp.dot(w.astype(BF16), split_heads(v_ref[hkeys, lanes[p]]),
                                                       preferred_element_type=F32)
                    spent = [jnp.broadcast_to(jnp.sum(jnp.where(head_h[h], cost_h[p], 0.0), axis=1, keepdims=True),
                                              (hr, tb)) for h in range(2)]
                    r = jnp.concatenate([r_h - jnp.concatenate(spent, axis=1), r[hr:]], axis=0)
                r_ref[p] = r
                r_top = r if r_top is None else jnp.maximum(r_top, r)
            st_ref[0] = jnp.max(r_top)

        if first and len(blocks) > 1:
            bounded = st_ref[1] < BOUNDED_SCORE
            pl.when(bounded)(functools.partial(finish, True))
            pl.when(jnp.logical_not(bounded))(functools.partial(finish, False))
        else:
            finish(False)
        return st_ref[0]

    def unfinished():
        return st_ref[0] + st_ref[1] > EXP2_FLOOR

    has_third = i >= 2

    @pl.when(has_third)
    def _():
        visit([i, i - 1], [causal, None], first=True, head_block=i - 2)

    @pl.when(jnp.logical_and(has_third, unfinished()))
    def _():
        visit([i - 2], [late])

    @pl.when(jnp.logical_not(has_third))
    def _():
        visit([i], [causal], first=True)

    def more(c):
        j, r_max = c
        return jnp.logical_and(j >= 0, r_max + st_ref[1] > EXP2_FLOOR)

    def step(c):
        j, _ = c
        return j - 1, visit([j], [None])

    lax.while_loop(more, step, (jnp.where(has_third, i - 3, i - 1), st_ref[0]))
    o_ref[...] = acc_ref[...].astype(o_ref.dtype)


def _neg_suffix_matrix(keys):
    idx = np.arange(2 * keys)
    later_same_head = (idx[:, None] // keys == idx[None, :] // keys) & (idx[:, None] >= idx[None, :])
    m = np.where(later_same_head, -1.0, 0.0).astype(np.float32)
    return np.concatenate([m, m], axis=0).astype(BF16)


def _attention(q, k, v, norms, batch):
    n = q.shape[0]
    seq = n // batch
    tile = n // norms.shape[0]
    tb = ATT_BLOCK
    nq = seq // tb
    width = ATT_PAIRS * LANES
    ntri = _neg_suffix_matrix(tb)
    ntrih = _neg_suffix_matrix(HEAD_KEYS)
    qn_spec = pl.BlockSpec((None, 2, LANES), lambda b, h, i: ((b * nq + i) * tb // tile, 0, 0))
    kn_spec = pl.BlockSpec((seq // tile, 2, LANES), lambda b, h, i: (b, 0, 0))
    kv_spec = pl.BlockSpec((seq, width), lambda b, h, i: (b, h))
    q_spec = pl.BlockSpec((tb, width), lambda b, h, i: (b * nq + i, h))
    return pl.pallas_call(
        _attn_kernel,
        grid=(batch, D_MODEL // width, nq),
        in_specs=[q_spec, kv_spec, kv_spec, qn_spec, kn_spec, _resident(ntri.shape), _resident(ntrih.shape)],
        out_specs=q_spec,
        out_shape=jax.ShapeDtypeStruct((n, D_MODEL), BF16),
        scratch_shapes=[pltpu.VMEM((tb, width), F32), pltpu.VMEM((ATT_PAIRS, tb, 2 * tb), F32),
                        pltpu.SMEM((2,), F32)],
        compiler_params=_params("parallel", "parallel", "arbitrary"),
        name="sb_attention",
    )(q, k, v, norms, norms, ntri, ntrih)


def kernel(x, s5_w_in, s5_lambda_re, s5_lambda_im, s5_b_re, s5_b_im, s5_c_re, s5_c_im, s5_d, s5_log_step,
           s5_w_glu, s5_b_glu, s5_w_out, sb_w_kv, sb_w_q, sb_w_out, mlp_w1, mlp_b1, mlp_w2, mlp_b2,
           ln_mix_g, ln_mix_b, ln_mlp_g, ln_mlp_b):
    batch, seq, d = x.shape
    n = batch * seq
    bf = lambda a: a.astype(BF16)
    x0 = x.reshape(n, d)
    w1, w2 = bf(mlp_w1), bf(mlp_w2)

    car, cai, wr, wi, pwr, pwi = _s5_prep(s5_lambda_re[0], s5_lambda_im[0], s5_log_step[0],
                                          s5_b_re[0], s5_b_im[0], s5_c_re[0], s5_c_im[0])
    d_tiled = jnp.tile(s5_d[0].reshape(N_TILES, 1, LANES), (1, 1, KB))
    u = _u_proj(x0, bf(s5_w_in[0]))
    y = _ssm(u, car, cai, wr, wi, pwr, pwi, d_tiled, batch)
    x1 = _layer_tail(
        y, x0,
        [bf(s5_w_glu[0]), s5_b_glu[0], bf(s5_w_out[0]), ln_mix_g[0], ln_mix_b[0],
         (w1, 0), mlp_b1[0], (w2, 0), mlp_b2[0], ln_mlp_g[0], ln_mlp_b[0]],
        glu=True)

    q, k, v, norms = _qkv(x1, bf(jnp.concatenate([sb_w_q[0], sb_w_kv], axis=1)))
    o = _attention(q, k, v, norms, batch)
    out = _layer_tail(
        o, x1,
        [bf(sb_w_out[0]), ln_mix_g[1], ln_mix_b[1],
         (w1, 1), mlp_b1[1], (w2, 1), mlp_b2[1], ln_mlp_g[1], ln_mlp_b[1]],
        glu=False)
    return out.reshape(batch, seq, d)
```

```python
import functools
import math

import jax
import jax.numpy as jnp
import numpy as np
from jax import lax
from jax.experimental import pallas as pl
from jax.experimental.pallas import tpu as pltpu

F32 = jnp.float32
BF16 = jnp.bfloat16

D_MODEL = 1024
DEPTH = 2
S5_GROUP = 16
S5_GROUPS = D_MODEL // S5_GROUP
S5_STATE = 64
SB_HEADS = 16
SB_HEAD_DIM = D_MODEL // SB_HEADS
D_FF = 4 * D_MODEL
DEEPNORM_ALPHA = (2.0 * DEPTH) ** 0.25
LN_EPS = 1e-5

LANES = 128
SUBLANES = 8
KB = 16
GROUPS_PER_TILE = LANES // S5_GROUP
N_TILES = D_MODEL // LANES
TILE_STATE = GROUPS_PER_TILE * S5_STATE
ALL_STATE = S5_GROUPS * S5_STATE
VMEM_LIMIT = 56 * 1024 * 1024
TAIL_SPLIT = 2
ATT_BLOCK = 128
ATT_PAIRS = 8
QKV_ROWS = 1024
HEAD_ROWS = 64
HEAD_KEYS = 64
EXP2_FLOOR = -130.0
BOUNDED_SCORE = 120.0
LOG2_E = 1.0 / math.log(2.0)
NT_DIMS = (((1,), (1,)), ((), ()))


def _params(*sem):
    return pltpu.CompilerParams(dimension_semantics=sem, vmem_limit_bytes=VMEM_LIMIT)


def _resident(shape, layer=None):
    zeros = (0,) * len(shape)
    if layer is None:
        return pl.BlockSpec(shape, lambda *_: zeros, pipeline_mode=pl.Buffered(1))
    return pl.BlockSpec((None,) + shape, lambda *_: (layer,) + zeros, pipeline_mode=pl.Buffered(1))


def _cmul(ar, ai, br, bi):
    return ar * br - ai * bi, ar * bi + ai * br


def _s5_prep_kernel(lr_ref, li_ref, ls_ref, br_ref, bi_ref, cr_ref, ci_ref,
                    car_ref, cai_ref, wr_ref, wi_ref, pwr_ref, pwi_ref,
                    a1r_ref, a1i_ref, pr_ref, pi_ref, bbr_ref, bbi_ref):
    tau = pl.program_id(0)

    @pl.when(tau == 0)
    def _():
        lr = lr_ref[...]
        li = li_ref[...]
        dt = jnp.exp(ls_ref[...])
        mag = jnp.exp(lr * dt)
        ang = li * dt
        a_re = mag * jnp.cos(ang)
        a_im = mag * jnp.sin(ang)
        nr = a_re - 1.0
        ni = a_im
        den = lr * lr + li * li
        f_re = (nr * lr + ni * li) / den
        f_im = (ni * lr - nr * li) / den
        bbr, bbi = _cmul(f_re, f_im, br_ref[...], bi_ref[...])
        bbr_ref[...] = bbr
        bbi_ref[...] = bbi
        a1r_ref[...] = a_re
        a1i_ref[...] = a_im
        pr_ref[...] = jnp.ones_like(a_re)
        pi_ref[...] = jnp.zeros_like(a_im)

    p_re = pr_ref[...]
    p_im = pi_ref[...]
    car, cai = _cmul(cr_ref[...], ci_ref[...], p_re, p_im)
    car_ref[...] = car
    cai_ref[...] = -cai
    wr, wi = _cmul(p_re, p_im, bbr_ref[...], bbi_ref[...])
    wr_ref[...] = wr
    wi_ref[...] = wi

    @pl.when(tau == KB)
    def _():
        q_re, q_im = p_re, p_im
        for i in range(SUBLANES):
            pwr_ref[i:i + 1, :] = q_re
            pwi_ref[i:i + 1, :] = q_im
            q_re, q_im = _cmul(q_re, q_im, p_re, p_im)

    n_re, n_im = _cmul(p_re, p_im, a1r_ref[...], a1i_ref[...])
    pr_ref[...] = n_re
    pi_ref[...] = n_im


def _s5_prep(lam_re, lam_im, log_step, b_re, b_im, c_re, c_im):
    g, p, h = S5_GROUPS, S5_STATE, S5_GROUP
    lr = lam_re.reshape(1, ALL_STATE)
    li = lam_im.reshape(1, ALL_STATE)
    ls = jnp.broadcast_to(log_step[:, None], (g, p)).reshape(1, ALL_STATE)
    br = b_re.transpose(2, 0, 1).reshape(h, ALL_STATE)
    bi = b_im.transpose(2, 0, 1).reshape(h, ALL_STATE)
    cr = c_re.transpose(1, 0, 2).reshape(h, ALL_STATE)
    ci = c_im.transpose(1, 0, 2).reshape(h, ALL_STATE)
    small = pl.BlockSpec((1, ALL_STATE), lambda t: (0, 0))
    big = pl.BlockSpec((h, ALL_STATE), lambda t: (0, 0))
    step = pl.BlockSpec((None, h, ALL_STATE), lambda t: (t, 0, 0))
    pw = pl.BlockSpec((SUBLANES, ALL_STATE), lambda t: (0, 0))
    out3 = jax.ShapeDtypeStruct((KB + 1, h, ALL_STATE), F32)
    outp = jax.ShapeDtypeStruct((SUBLANES, ALL_STATE), F32)
    return pl.pallas_call(
        _s5_prep_kernel,
        grid=(KB + 1,),
        in_specs=[small, small, small, big, big, big, big],
        out_specs=[step, step, step, step, pw, pw],
        out_shape=[out3, out3, out3, out3, outp, outp],
        scratch_shapes=[pltpu.VMEM((1, ALL_STATE), F32)] * 4 + [pltpu.VMEM((h, ALL_STATE), F32)] * 2,
        compiler_params=_params("arbitrary"),
        name="s5_prep",
    )(lr, li, ls, br, bi, cr, ci)


def _uproj_kernel(x_ref, w_ref, o_ref):
    o_ref[...] = jnp.dot(x_ref[...].astype(BF16), w_ref[...], preferred_element_type=F32)


def _u_proj(x, w_in):
    n = x.shape[0]
    tm = min(n, 1024)
    row = pl.BlockSpec((tm, D_MODEL), lambda i: (i, 0))
    return pl.pallas_call(
        _uproj_kernel,
        grid=(n // tm,),
        in_specs=[row, _resident((D_MODEL, D_MODEL))],
        out_specs=row,
        out_shape=jax.ShapeDtypeStruct((n, D_MODEL), F32),
        compiler_params=_params("parallel"),
        name="s5_u_proj",
    )(x, w_in)


def _shift_rows(x, k, row):
    return jnp.where(row >= k, pltpu.roll(x, k, 0), 0.0)


def _ssm_build_operators(car_ref, cai_ref, wr_ref, wi_ref, t_ref, win_ref, voutt_ref):
    ts = TILE_STATE
    row_group = lax.broadcasted_iota(jnp.int32, (LANES, ts), 0) // S5_GROUP
    lane_group = lax.broadcasted_iota(jnp.int32, (LANES, ts), 1) // S5_STATE
    same_group = row_group == lane_group

    def expand(re16, im16):
        tile = lambda a: jnp.where(same_group, jnp.concatenate([a] * GROUPS_PER_TILE, axis=0), 0.0)
        return jnp.concatenate([tile(re16), tile(im16)], axis=1)

    t_ref[...] = jnp.zeros_like(t_ref)
    bbar = expand(wr_ref[0], wi_ref[0]).astype(BF16)
    for tau in range(KB + 1):
        ca = expand(car_ref[tau], cai_ref[tau]).astype(BF16)
        if tau >= 1:
            voutt_ref[(tau - 1) * LANES:tau * LANES, :] = ca
        if tau < KB:
            lag = lax.dot_general(bbar, ca, NT_DIMS, preferred_element_type=F32).astype(BF16)
            for j in range(KB - tau):
                t_ref[j * LANES:(j + 1) * LANES, (j + tau) * LANES:(j + tau + 1) * LANES] = lag
            win_ref[(KB - 1 - tau) * LANES:(KB - tau) * LANES, :] = expand(wr_ref[tau], wi_ref[tau]).astype(BF16)


def _ssm_kernel(u_ref, car_ref, cai_ref, wr_ref, wi_ref, pwr_ref, pwi_ref, d_ref, y_ref,
                s_ref, yin_ref, t_ref, win_ref, voutt_ref):
    nb = u_ref.shape[0] // KB
    ts = TILE_STATE

    @pl.when(pl.program_id(1) == 0)
    def _():
        _ssm_build_operators(car_ref, cai_ref, wr_ref, wi_ref, t_ref, win_ref, voutt_ref)

    u_cat = jnp.concatenate([u_ref[pl.ds(j, nb, stride=KB), :] for j in range(KB)], axis=1)
    u_bf = u_cat.astype(BF16)
    s_ref[...] = jnp.dot(u_bf, win_ref[...], preferred_element_type=F32)
    width = 2 * LANES
    chunks = [slice(c * width, (c + 1) * width) for c in range(KB * LANES // width)]
    row = lax.broadcasted_iota(jnp.int32, (SUBLANES, ts), 0)
    pw_re = pwr_ref[...]
    pw_im = pwi_ref[...]

    def group(r, carry):
        c_re, c_im = carry
        rows = pl.ds(r * SUBLANES, SUBLANES)
        x_re = s_ref[rows, 0:ts]
        x_im = s_ref[rows, ts:2 * ts]
        for k in (1, 2, 4):
            a_re = pw_re[k - 1:k]
            a_im = pw_im[k - 1:k]
            sh_re = _shift_rows(x_re, k, row)
            sh_im = _shift_rows(x_im, k, row)
            x_re, x_im = x_re + a_re * sh_re - a_im * sh_im, x_im + a_re * sh_im + a_im * sh_re
        x_re, x_im = x_re + pw_re * c_re - pw_im * c_im, x_im + pw_re * c_im + pw_im * c_re
        s_ref[rows, 0:ts] = jnp.where(row >= 1, pltpu.roll(x_re, 1, 0), c_re)
        s_ref[rows, ts:2 * ts] = jnp.where(row >= 1, pltpu.roll(x_im, 1, 0), c_im)
        return x_re[SUBLANES - 1:SUBLANES], x_im[SUBLANES - 1:SUBLANES]

    groups = nb // SUBLANES
    carry = (jnp.zeros((1, ts), F32),) * 2
    for r in range(groups):
        for c, cols in enumerate(chunks):
            if c * groups // len(chunks) == r:
                yin_ref[:, cols] = jnp.dot(u_bf[:, :cols.stop], t_ref[0:cols.stop, cols],
                                           preferred_element_type=F32)
        carry = group(r, carry)
    s_bf = s_ref[...].astype(BF16)
    for c, cols in enumerate(chunks):
        y = yin_ref[:, cols] + lax.dot_general(s_bf, voutt_ref[cols, :], NT_DIMS, preferred_element_type=F32)
        y = y + d_ref[:, cols] * u_cat[:, cols]
        for jj in range(width // LANES):
            y_ref[pl.ds(c * (width // LANES) + jj, nb, stride=KB), :] = y[:, jj * LANES:(jj + 1) * LANES]


def _ssm(u, car, cai, wr, wi, pwr, pwi, d_tiled, batch):
    seq = u.shape[0] // batch
    nb = seq // KB
    factor = pl.BlockSpec((KB + 1, S5_GROUP, TILE_STATE), lambda c, b: (0, 0, c))
    power = pl.BlockSpec((SUBLANES, TILE_STATE), lambda c, b: (0, c))
    io = pl.BlockSpec((seq, LANES), lambda c, b: (b, c))
    return pl.pallas_call(
        _ssm_kernel,
        grid=(N_TILES, batch),
        in_specs=[io, factor, factor, factor, factor, power, power,
                  pl.BlockSpec((None, 1, KB * LANES), lambda c, b: (c, 0, 0))],
        out_specs=io,
        out_shape=jax.ShapeDtypeStruct(u.shape, F32),
        scratch_shapes=[pltpu.VMEM((nb, 2 * TILE_STATE), F32),
                        pltpu.VMEM((nb, KB * LANES), F32),
                        pltpu.VMEM((KB * LANES, KB * LANES), BF16),
                        pltpu.VMEM((KB * LANES, 2 * TILE_STATE), BF16),
                        pltpu.VMEM((KB * LANES, 2 * TILE_STATE), BF16)],
        compiler_params=_params("parallel", "arbitrary"),
        name="s5_ssm",
    )(u, car, cai, wr, wi, pwr, pwi, d_tiled)


def _layer_norm(z, g, b):
    mu = jnp.mean(z, axis=-1, keepdims=True)
    zc = z - mu
    var = jnp.mean(zc * zc, axis=-1, keepdims=True)
    return zc * lax.rsqrt(var + LN_EPS) * g + b


def _tail_kernel(*refs, glu):
    if glu:
        (m_ref, x_ref, wglu_ref, bglu_ref, wout_ref, g1_ref, b1n_ref,
         w1_ref, b1_ref, w2_ref, b2_ref, g2_ref, b2n_ref, o_ref) = refs
    else:
        (m_ref, x_ref, wout_ref, g1_ref, b1n_ref,
         w1_ref, b1_ref, w2_ref, b2_ref, g2_ref, b2n_ref, o_ref) = refs
    tm = x_ref.shape[0]
    parts = [slice(h * tm // TAIL_SPLIT, (h + 1) * tm // TAIL_SPLIT) for h in range(TAIL_SPLIT)]
    mm = lambda a, b: jnp.dot(a, b, preferred_element_type=F32)
    if glu:
        g = [jax.nn.gelu(m_ref[rows, :]) for rows in parts]
        gate = [mm(gh.astype(BF16), wglu_ref[...]) + bglu_ref[...] for gh in g]
        m = [(gh * jax.nn.sigmoid(th)).astype(BF16) for gh, th in zip(g, gate)]
    else:
        m = [m_ref[rows, :] for rows in parts]
    mix = [mm(mh, wout_ref[...]) for mh in m]
    x1 = [_layer_norm(DEEPNORM_ALPHA * x_ref[rows, :] + mh, g1_ref[...], b1n_ref[...])
          for rows, mh in zip(parts, mix)]
    x1_bf = [a.astype(BF16) for a in x1]
    ff = [None] * TAIL_SPLIT
    for c in range(D_FF // D_MODEL):
        cols = slice(c * D_MODEL, (c + 1) * D_MODEL)
        w1c = w1_ref[:, cols].astype(BF16)
        hid = [jnp.square(jnp.maximum(mm(a, w1c) + b1_ref[:, cols], 0.0)) for a in x1_bf]
        for h in range(TAIL_SPLIT):
            out = mm(hid[h].astype(BF16), w2_ref[cols, :])
            ff[h] = out if c == 0 else ff[h] + out
    for h, rows in enumerate(parts):
        o_ref[rows, :] = _layer_norm(DEEPNORM_ALPHA * x1[h] + ff[h] + b2_ref[...], g2_ref[...], b2n_ref[...])


def _layer_tail(m, x, weights, glu):
    n = x.shape[0]
    tm = min(n, 512)
    row = pl.BlockSpec((tm, D_MODEL), lambda i: (i, 0))
    ops = [m, x]
    specs = [row, row]
    for wgt in weights:
        if isinstance(wgt, tuple):
            ops.append(wgt[0])
            specs.append(_resident(wgt[0].shape[1:], layer=wgt[1]))
        else:
            a = wgt.reshape(1, -1) if wgt.ndim == 1 else wgt
            ops.append(a)
            specs.append(_resident(a.shape))
    return pl.pallas_call(
        functools.partial(_tail_kernel, glu=glu),
        grid=(n // tm,),
        in_specs=specs,
        out_specs=row,
        out_shape=jax.ShapeDtypeStruct((n, D_MODEL), F32),
        compiler_params=_params("parallel"),
        name="layer_tail_glu" if glu else "layer_tail",
    )(*ops)


def _qkv_kernel(x_ref, w_ref, hsum_ref, q_ref, k_ref, v_ref, norm_ref):
    x = x_ref[...].astype(BF16)
    y = jnp.dot(x, w_ref[:, :2 * D_MODEL], preferred_element_type=F32)
    q = (y[:, :D_MODEL] * (LOG2_E / math.sqrt(SB_HEAD_DIM))).astype(BF16)
    k = y[:, D_MODEL:].astype(BF16)
    q_ref[...] = q
    k_ref[...] = k
    v_ref[...] = jnp.dot(x, w_ref[:, 2 * D_MODEL:], preferred_element_type=F32).astype(BF16)

    def head_norm_max(a):
        sq = jnp.square(a.astype(F32)).astype(BF16)
        return jnp.max(jnp.dot(sq, hsum_ref[...], preferred_element_type=F32), axis=0, keepdims=True)

    norm_ref[0:1, :] = head_norm_max(q)
    norm_ref[1:2, :] = head_norm_max(k)


def _qkv(x, w_qkv):
    n = x.shape[0]
    tm = min(n, QKV_ROWS)
    row = pl.BlockSpec((tm, D_MODEL), lambda i: (i, 0))
    out = jax.ShapeDtypeStruct((n, D_MODEL), BF16)
    hsum = (np.arange(D_MODEL)[:, None] // SB_HEAD_DIM == np.arange(LANES)[None, :]).astype(BF16)
    return pl.pallas_call(
        _qkv_kernel,
        grid=(n // tm,),
        in_specs=[row, _resident(w_qkv.shape), _resident(hsum.shape)],
        out_specs=[row, row, row, pl.BlockSpec((None, 2, LANES), lambda i: (i, 0, 0))],
        out_shape=[out, out, out, jax.ShapeDtypeStruct((n // tm, 2, LANES), F32)],
        compiler_params=_params("parallel"),
        name="qkv_proj",
    )(x, w_qkv, hsum)


def _attn_kernel(q_ref, k_ref, v_ref, qn_ref, kn_ref, ntri_ref, ntrih_ref, o_ref, acc_ref, r_ref, st_ref):
    i = pl.program_id(2)
    tb = ATT_BLOCK
    pairs = q_ref.shape[1] // LANES
    lane = lax.broadcasted_iota(jnp.int32, (tb, LANES), 1)
    first = lane < SB_HEAD_DIM

    def split_heads(a):
        zero = jnp.zeros_like(a)
        mine = lax.broadcasted_iota(jnp.int32, a.shape, 1) < SB_HEAD_DIM
        return jnp.concatenate([jnp.where(mine, a, zero), jnp.where(mine, zero, a)], axis=0)

    st_ref[1] = jnp.sqrt(jnp.max(qn_ref[0:1, :]) * jnp.max(kn_ref[:, 1:2, :])) * 1.02 + 1e-3

    rr = lax.broadcasted_iota(jnp.int32, (tb, 2 * tb), 0)
    cc = lax.broadcasted_iota(jnp.int32, (tb, 2 * tb), 1)
    causal = jnp.bitwise_and(cc, tb - 1) < rr
    lanes = [slice(p * LANES, (p + 1) * LANES) for p in range(pairs)]

    hr, hk = HEAD_ROWS, HEAD_KEYS
    late = jnp.logical_not(jnp.logical_and(rr < hr, jnp.bitwise_and(cc, tb - 1) >= tb - hk))
    lane_h = lax.broadcasted_iota(jnp.int32, (hr, 2 * hk), 1)
    head_h = [lane_h < hk, lane_h >= hk]

    def cost_of(z, no_overflow):
        if no_overflow:
            return jnp.log2(1.0 + jnp.exp2(z))
        return jnp.maximum(z, 0.0) + jnp.log2(1.0 + 1.0 / jnp.exp2(jnp.abs(z)))

    def hi_lo(c):
        hi = c.astype(BF16)
        return jnp.concatenate([hi, (c - hi.astype(F32)).astype(BF16)], axis=1)

    def visit(blocks, masks, first=False, head_block=None):
        keys = [pl.ds(pl.multiple_of(j * tb, tb), tb) for j in blocks]
        todo = [(p, t) for p in range(pairs) for t in range(len(blocks))]
        z = {pt: lax.dot_general(q_ref[:, lanes[pt[0]]], split_heads(k_ref[keys[pt[1]], lanes[pt[0]]]), NT_DIMS,
                                 preferred_element_type=F32) for pt in todo}
        if head_block is not None:
            hkeys = pl.ds(pl.multiple_of(head_block * tb + (tb - hk), hk), hk)
            zh = [lax.dot_general(q_ref[0:hr, lanes[p]], split_heads(k_ref[hkeys, lanes[p]]), NT_DIMS,
                                  preferred_element_type=F32) for p in range(pairs)]

        def finish(no_overflow):
            cost = {}
            for pt in todo:
                c = cost_of(z[pt], no_overflow)
                cost[pt] = c if masks[pt[1]] is None else jnp.where(masks[pt[1]], c, 0.0)
            suffix = {pt: jnp.dot(hi_lo(cost[pt]), ntri_ref[...], preferred_element_type=F32) for pt in todo}
            if head_block is not None:
                cost_h = [cost_of(zh[p], no_overflow) for p in range(pairs)]
                suffix_h = [jnp.dot(hi_lo(cost_h[p]), ntrih_ref[...], preferred_element_type=F32)
                            for p in range(pairs)]
            r_top = None
            for p in range(pairs):
                r = jnp.zeros((tb, 2 * tb), F32) if first else r_ref[p]
                weights = []
                for t in range(len(blocks)):
                    w = jnp.exp2(z[p, t] + suffix[p, t] + r)
                    if masks[t] is not None:
                        w = jnp.where(masks[t], w, 0.0)
                    weights.append(w.astype(BF16))
                    spent = [jnp.broadcast_to(jnp.sum(cost[p, t][:, h * tb:(h + 1) * tb], axis=1, keepdims=True),
                                              (tb, tb)) for h in range(2)]
                    r = r - jnp.concatenate(spent, axis=1)
                values = jnp.concatenate([split_heads(v_ref[keys[t], lanes[p]]) for t in range(len(blocks))], axis=0)
                out = jnp.dot(jnp.concatenate(weights, axis=1), values, preferred_element_type=F32)
                acc_ref[:, lanes[p]] = out if first else acc_ref[:, lanes[p]] + out
                if head_block is not None:
                    r_h = r[0:hr]
                    w = jnp.exp2(zh[p] + suffix_h[p] + jnp.where(head_h[0], r_h[:, 0:LANES], r_h[:, tb:tb + LANES]))
                    acc_ref[0:hr, lanes[p]] += jnp.dot(w.astype(BF16), split_heads(v_ref[hkeys, lanes[p]]),
                                                       preferred_element_type=F32)
                    spent = [jnp.broadcast_to(jnp.sum(jnp.where(head_h[h], cost_h[p], 0.0), axis=1, keepdims=True),
                                              (hr, tb)) for h in range(2)]
                    r = jnp.concatenate([r_h - jnp.concatenate(spent, axis=1), r[hr:]], axis=0)
                r_ref[p] = r
                r_top = r if r_top is None else jnp.maximum(r_top, r)
            st_ref[0] = jnp.max(r_top)

        if first and len(blocks) > 1:
            bounded = st_ref[1] < BOUNDED_SCORE
            pl.when(bounded)(functools.partial(finish, True))
            pl.when(jnp.logical_not(bounded))(functools.partial(finish, False))
        else:
            finish(False)
        return st_ref[0]

    def unfinished():
        return st_ref[0] + st_ref[1] > EXP2_FLOOR

    has_third = i >= 2

    @pl.when(has_third)
    def _():
        visit([i, i - 1], [causal, None], first=True, head_block=i - 2)

    @pl.when(jnp.logical_and(has_third, unfinished()))
    def _():
        visit([i - 2], [late])

    @pl.when(jnp.logical_not(has_third))
    def _():
        visit([i], [causal], first=True)

    def more(c):
        j, r_max = c
        return jnp.logical_and(j >= 0, r_max + st_ref[1] > EXP2_FLOOR)

    def step(c):
        j, _ = c
        return j - 1, visit([j], [None])

    lax.while_loop(more, step, (jnp.where(has_third, i - 3, i - 1), st_ref[0]))
    o_ref[...] = acc_ref[...].astype(o_ref.dtype)


def _neg_suffix_matrix(keys):
    idx = np.arange(2 * keys)
    later_same_head = (idx[:, None] // keys == idx[None, :] // keys) & (idx[:, None] >= idx[None, :])
    m = np.where(later_same_head, -1.0, 0.0).astype(np.float32)
    return np.concatenate([m, m], axis=0).astype(BF16)


def _attention(q, k, v, norms, batch):
    n = q.shape[0]
    seq = n // batch
    tile = n // norms.shape[0]
    tb = ATT_BLOCK
    nq = seq // tb
    width = ATT_PAIRS * LANES
    ntri = _neg_suffix_matrix(tb)
    ntrih = _neg_suffix_matrix(HEAD_KEYS)
    qn_spec = pl.BlockSpec((None, 2, LANES), lambda b, h, i: ((b * nq + i) * tb // tile, 0, 0))
    kn_spec = pl.BlockSpec((seq // tile, 2, LANES), lambda b, h, i: (b, 0, 0))
    kv_spec = pl.BlockSpec((seq, width), lambda b, h, i: (b, h))
    q_spec = pl.BlockSpec((tb, width), lambda b, h, i: (b * nq + i, h))
    return pl.pallas_call(
        _attn_kernel,
        grid=(batch, D_MODEL // width, nq),
        in_specs=[q_spec, kv_spec, kv_spec, qn_spec, kn_spec, _resident(ntri.shape), _resident(ntrih.shape)],
        out_specs=q_spec,
        out_shape=jax.ShapeDtypeStruct((n, D_MODEL), BF16),
        scratch_shapes=[pltpu.VMEM((tb, width), F32), pltpu.VMEM((ATT_PAIRS, tb, 2 * tb), F32),
                        pltpu.SMEM((2,), F32)],
        compiler_params=_params("parallel", "parallel", "arbitrary"),
        name="sb_attention",
    )(q, k, v, norms, norms, ntri, ntrih)


def kernel(x, s5_w_in, s5_lambda_re, s5_lambda_im, s5_b_re, s5_b_im, s5_c_re, s5_c_im, s5_d, s5_log_step,
           s5_w_glu, s5_b_glu, s5_w_out, sb_w_kv, sb_w_q, sb_w_out, mlp_w1, mlp_b1, mlp_w2, mlp_b2,
           ln_mix_g, ln_mix_b, ln_mlp_g, ln_mlp_b):
    batch, seq, d = x.shape
    n = batch * seq
    bf = lambda a: a.astype(BF16)
    x0 = x.reshape(n, d)
    w1, w2 = mlp_w1, bf(mlp_w2)

    car, cai, wr, wi, pwr, pwi = _s5_prep(s5_lambda_re[0], s5_lambda_im[0], s5_log_step[0],
                                          s5_b_re[0], s5_b_im[0], s5_c_re[0], s5_c_im[0])
    d_tiled = jnp.tile(s5_d[0].reshape(N_TILES, 1, LANES), (1, 1, KB))
    u = _u_proj(x0, bf(s5_w_in[0]))
    y = _ssm(u, car, cai, wr, wi, pwr, pwi, d_tiled, batch)
    x1 = _layer_tail(
        y, x0,
        [bf(s5_w_glu[0]), s5_b_glu[0], bf(s5_w_out[0]), ln_mix_g[0], ln_mix_b[0],
         (w1, 0), mlp_b1[0], (w2, 0), mlp_b2[0], ln_mlp_g[0], ln_mlp_b[0]],
        glu=True)

    q, k, v, norms = _qkv(x1, bf(jnp.concatenate([sb_w_q[0], sb_w_kv], axis=1)))
    o = _attention(q, k, v, norms, batch)
    out = _layer_tail(
        o, x1,
        [bf(sb_w_out[0]), ln_mix_g[1], ln_mix_b[1],
         (w1, 1), mlp_b1[1], (w2, 1), mlp_b2[1], ln_mlp_g[1], ln_mlp_b[1]],
        glu=False)
    return out.reshape(batch, seq, d)
```

```python
import functools
import math

import jax
import jax.numpy as jnp
import numpy as np
from jax import lax
from jax.experimental import pallas as pl
from jax.experimental.pallas import tpu as pltpu

F32 = jnp.float32
BF16 = jnp.bfloat16

D_MODEL = 1024
DEPTH = 2
S5_GROUP = 16
S5_GROUPS = D_MODEL // S5_GROUP
S5_STATE = 64
SB_HEADS = 16
SB_HEAD_DIM = D_MODEL // SB_HEADS
D_FF = 4 * D_MODEL
DEEPNORM_ALPHA = (2.0 * DEPTH) ** 0.25
LN_EPS = 1e-5

LANES = 128
SUBLANES = 8
KB = 16
GROUPS_PER_TILE = LANES // S5_GROUP
N_TILES = D_MODEL // LANES
TILE_STATE = GROUPS_PER_TILE * S5_STATE
ALL_STATE = S5_GROUPS * S5_STATE
VMEM_LIMIT = 56 * 1024 * 1024
TAIL_SPLIT = 2
ATT_BLOCK = 128
ATT_PAIRS = 8
QKV_ROWS = 1024
HEAD_ROWS = 64
HEAD_KEYS = 64
EXP2_FLOOR = -130.0
BOUNDED_SCORE = 120.0
LOG2_E = 1.0 / math.log(2.0)
NT_DIMS = (((1,), (1,)), ((), ()))


def _params(*sem):
    return pltpu.CompilerParams(dimension_semantics=sem, vmem_limit_bytes=VMEM_LIMIT)


def _resident(shape, layer=None):
    zeros = (0,) * len(shape)
    if layer is None:
        return pl.BlockSpec(shape, lambda *_: zeros, pipeline_mode=pl.Buffered(1))
    return pl.BlockSpec((None,) + shape, lambda *_: (layer,) + zeros, pipeline_mode=pl.Buffered(1))


def _cmul(ar, ai, br, bi):
    return ar * br - ai * bi, ar * bi + ai * br


def _s5_prep_kernel(lr_ref, li_ref, ls_ref, br_ref, bi_ref, cr_ref, ci_ref,
                    car_ref, cai_ref, wr_ref, wi_ref, pwr_ref, pwi_ref,
                    a1r_ref, a1i_ref, pr_ref, pi_ref, bbr_ref, bbi_ref):
    tau = pl.program_id(0)

    @pl.when(tau == 0)
    def _():
        lr = lr_ref[...]
        li = li_ref[...]
        dt = jnp.exp(ls_ref[...])
        mag = jnp.exp(lr * dt)
        ang = li * dt
        a_re = mag * jnp.cos(ang)
        a_im = mag * jnp.sin(ang)
        nr = a_re - 1.0
        ni = a_im
        den = lr * lr + li * li
        f_re = (nr * lr + ni * li) / den
        f_im = (ni * lr - nr * li) / den
        bbr, bbi = _cmul(f_re, f_im, br_ref[...], bi_ref[...])
        bbr_ref[...] = bbr
        bbi_ref[...] = bbi
        a1r_ref[...] = a_re
        a1i_ref[...] = a_im
        pr_ref[...] = jnp.ones_like(a_re)
        pi_ref[...] = jnp.zeros_like(a_im)

    p_re = pr_ref[...]
    p_im = pi_ref[...]
    car, cai = _cmul(cr_ref[...], ci_ref[...], p_re, p_im)
    car_ref[...] = car
    cai_ref[...] = -cai
    wr, wi = _cmul(p_re, p_im, bbr_ref[...], bbi_ref[...])
    wr_ref[...] = wr
    wi_ref[...] = wi

    @pl.when(tau == KB)
    def _():
        q_re, q_im = p_re, p_im
        for i in range(SUBLANES):
            pwr_ref[i:i + 1, :] = q_re
            pwi_ref[i:i + 1, :] = q_im
            q_re, q_im = _cmul(q_re, q_im, p_re, p_im)

    n_re, n_im = _cmul(p_re, p_im, a1r_ref[...], a1i_ref[...])
    pr_ref[...] = n_re
    pi_ref[...] = n_im


def _s5_prep(lam_re, lam_im, log_step, b_re, b_im, c_re, c_im):
    g, p, h = S5_GROUPS, S5_STATE, S5_GROUP
    lr = lam_re.reshape(1, ALL_STATE)
    li = lam_im.reshape(1, ALL_STATE)
    ls = jnp.broadcast_to(log_step[:, None], (g, p)).reshape(1, ALL_STATE)
    br = b_re.transpose(2, 0, 1).reshape(h, ALL_STATE)
    bi = b_im.transpose(2, 0, 1).reshape(h, ALL_STATE)
    cr = c_re.transpose(1, 0, 2).reshape(h, ALL_STATE)
    ci = c_im.transpose(1, 0, 2).reshape(h, ALL_STATE)
    small = pl.BlockSpec((1, ALL_STATE), lambda t: (0, 0))
    big = pl.BlockSpec((h, ALL_STATE), lambda t: (0, 0))
    step = pl.BlockSpec((None, h, ALL_STATE), lambda t: (t, 0, 0))
    pw = pl.BlockSpec((SUBLANES, ALL_STATE), lambda t: (0, 0))
    out3 = jax.ShapeDtypeStruct((KB + 1, h, ALL_STATE), F32)
    outp = jax.ShapeDtypeStruct((SUBLANES, ALL_STATE), F32)
    return pl.pallas_call(
        _s5_prep_kernel,
        grid=(KB + 1,),
        in_specs=[small, small, small, big, big, big, big],
        out_specs=[step, step, step, step, pw, pw],
        out_shape=[out3, out3, out3, out3, outp, outp],
        scratch_shapes=[pltpu.VMEM((1, ALL_STATE), F32)] * 4 + [pltpu.VMEM((h, ALL_STATE), F32)] * 2,
        compiler_params=_params("arbitrary"),
        name="s5_prep",
    )(lr, li, ls, br, bi, cr, ci)


def _uproj_kernel(x_ref, w_ref, o_ref):
    o_ref[...] = jnp.dot(x_ref[...].astype(BF16), w_ref[...], preferred_element_type=F32)


def _u_proj(x, w_in):
    n = x.shape[0]
    tm = min(n, 2048)
    row = pl.BlockSpec((tm, D_MODEL), lambda i: (i, 0))
    return pl.pallas_call(
        _uproj_kernel,
        grid=(n // tm,),
        in_specs=[row, _resident((D_MODEL, D_MODEL))],
        out_specs=row,
        out_shape=jax.ShapeDtypeStruct((n, D_MODEL), F32),
        compiler_params=_params("parallel"),
        name="s5_u_proj",
    )(x, w_in)


def _shift_rows(x, k, row):
    return jnp.where(row >= k, pltpu.roll(x, k, 0), 0.0)


def _ssm_build_operators(car_ref, cai_ref, wr_ref, wi_ref, t_ref, win_ref, voutt_ref):
    ts = TILE_STATE
    row_group = lax.broadcasted_iota(jnp.int32, (LANES, ts), 0) // S5_GROUP
    lane_group = lax.broadcasted_iota(jnp.int32, (LANES, ts), 1) // S5_STATE
    same_group = row_group == lane_group

    def expand(re16, im16):
        tile = lambda a: jnp.where(same_group, jnp.concatenate([a] * GROUPS_PER_TILE, axis=0), 0.0)
        return jnp.concatenate([tile(re16), tile(im16)], axis=1)

    t_ref[...] = jnp.zeros_like(t_ref)
    bbar = expand(wr_ref[0], wi_ref[0]).astype(BF16)
    for tau in range(KB + 1):
        ca = expand(car_ref[tau], cai_ref[tau]).astype(BF16)
        if tau >= 1:
            voutt_ref[(tau - 1) * LANES:tau * LANES, :] = ca
        if tau < KB:
            lag = lax.dot_general(bbar, ca, NT_DIMS, preferred_element_type=F32).astype(BF16)
            for j in range(KB - tau):
                t_ref[j * LANES:(j + 1) * LANES, (j + tau) * LANES:(j + tau + 1) * LANES] = lag
            win_ref[(KB - 1 - tau) * LANES:(KB - tau) * LANES, :] = expand(wr_ref[tau], wi_ref[tau]).astype(BF16)


def _ssm_kernel(u_ref, car_ref, cai_ref, wr_ref, wi_ref, pwr_ref, pwi_ref, d_ref, y_ref,
                s_ref, yin_ref, t_ref, win_ref, voutt_ref):
    nb = u_ref.shape[0] // KB
    ts = TILE_STATE

    @pl.when(pl.program_id(1) == 0)
    def _():
        _ssm_build_operators(car_ref, cai_ref, wr_ref, wi_ref, t_ref, win_ref, voutt_ref)

    u_cat = jnp.concatenate([u_ref[pl.ds(j, nb, stride=KB), :] for j in range(KB)], axis=1)
    u_bf = u_cat.astype(BF16)
    s_ref[...] = jnp.dot(u_bf, win_ref[...], preferred_element_type=F32)
    width = 2 * LANES
    chunks = [slice(c * width, (c + 1) * width) for c in range(KB * LANES // width)]
    row = lax.broadcasted_iota(jnp.int32, (SUBLANES, ts), 0)
    pw_re = pwr_ref[...]
    pw_im = pwi_ref[...]

    def group(r, carry):
        c_re, c_im = carry
        rows = pl.ds(r * SUBLANES, SUBLANES)
        x_re = s_ref[rows, 0:ts]
        x_im = s_ref[rows, ts:2 * ts]
        for k in (1, 2, 4):
            a_re = pw_re[k - 1:k]
            a_im = pw_im[k - 1:k]
            sh_re = _shift_rows(x_re, k, row)
            sh_im = _shift_rows(x_im, k, row)
            x_re, x_im = x_re + a_re * sh_re - a_im * sh_im, x_im + a_re * sh_im + a_im * sh_re
        x_re, x_im = x_re + pw_re * c_re - pw_im * c_im, x_im + pw_re * c_im + pw_im * c_re
        s_ref[rows, 0:ts] = jnp.where(row >= 1, pltpu.roll(x_re, 1, 0), c_re)
        s_ref[rows, ts:2 * ts] = jnp.where(row >= 1, pltpu.roll(x_im, 1, 0), c_im)
        return x_re[SUBLANES - 1:SUBLANES], x_im[SUBLANES - 1:SUBLANES]

    groups = nb // SUBLANES
    carry = (jnp.zeros((1, ts), F32),) * 2
    for r in range(groups):
        for c, cols in enumerate(chunks):
            if c * groups // len(chunks) == r:
                yin_ref[:, cols] = jnp.dot(u_bf[:, :cols.stop], t_ref[0:cols.stop, cols],
                                           preferred_element_type=F32)
        carry = group(r, carry)
    s_bf = s_ref[...].astype(BF16)
    for c, cols in enumerate(chunks):
        y = yin_ref[:, cols] + lax.dot_general(s_bf, voutt_ref[cols, :], NT_DIMS, preferred_element_type=F32)
        y = y + d_ref[:, cols] * u_cat[:, cols]
        for jj in range(width // LANES):
            y_ref[pl.ds(c * (width // LANES) + jj, nb, stride=KB), :] = y[:, jj * LANES:(jj + 1) * LANES]


def _ssm(u, car, cai, wr, wi, pwr, pwi, d_tiled, batch):
    seq = u.shape[0] // batch
    nb = seq // KB
    factor = pl.BlockSpec((KB + 1, S5_GROUP, TILE_STATE), lambda c, b: (0, 0, c))
    power = pl.BlockSpec((SUBLANES, TILE_STATE), lambda c, b: (0, c))
    io = pl.BlockSpec((seq, LANES), lambda c, b: (b, c))
    return pl.pallas_call(
        _ssm_kernel,
        grid=(N_TILES, batch),
        in_specs=[io, factor, factor, factor, factor, power, power,
                  pl.BlockSpec((None, 1, KB * LANES), lambda c, b: (c, 0, 0))],
        out_specs=io,
        out_shape=jax.ShapeDtypeStruct(u.shape, F32),
        scratch_shapes=[pltpu.VMEM((nb, 2 * TILE_STATE), F32),
                        pltpu.VMEM((nb, KB * LANES), F32),
                        pltpu.VMEM((KB * LANES, KB * LANES), BF16),
                        pltpu.VMEM((KB * LANES, 2 * TILE_STATE), BF16),
                        pltpu.VMEM((KB * LANES, 2 * TILE_STATE), BF16)],
        compiler_params=_params("parallel", "arbitrary"),
        name="s5_ssm",
    )(u, car, cai, wr, wi, pwr, pwi, d_tiled)


def _layer_norm(z, g, b):
    mu = jnp.mean(z, axis=-1, keepdims=True)
    zc = z - mu
    var = jnp.mean(zc * zc, axis=-1, keepdims=True)
    return zc * lax.rsqrt(var + LN_EPS) * g + b


def _tail_kernel(*refs, glu):
    if glu:
        (m_ref, x_ref, wglu_ref, bglu_ref, wout_ref, g1_ref, b1n_ref,
         w1_ref, b1_ref, w2_ref, b2_ref, g2_ref, b2n_ref, o_ref) = refs
    else:
        (m_ref, x_ref, wout_ref, g1_ref, b1n_ref,
         w1_ref, b1_ref, w2_ref, b2_ref, g2_ref, b2n_ref, o_ref) = refs
    tm = x_ref.shape[0]
    parts = [slice(h * tm // TAIL_SPLIT, (h + 1) * tm // TAIL_SPLIT) for h in range(TAIL_SPLIT)]
    mm = lambda a, b: jnp.dot(a, b, preferred_element_type=F32)
    if glu:
        g = [jax.nn.gelu(m_ref[rows, :]) for rows in parts]
        gate = [mm(gh.astype(BF16), wglu_ref[...]) + bglu_ref[...] for gh in g]
        m = [(gh * jax.nn.sigmoid(th)).astype(BF16) for gh, th in zip(g, gate)]
    else:
        m = [m_ref[rows, :] for rows in parts]
    mix = [mm(mh, wout_ref[...]) for mh in m]
    x1 = [_layer_norm(DEEPNORM_ALPHA * x_ref[rows, :] + mh, g1_ref[...], b1n_ref[...])
          for rows, mh in zip(parts, mix)]
    x1_bf = [a.astype(BF16) for a in x1]
    ff = [None] * TAIL_SPLIT
    for c in range(D_FF // D_MODEL):
        cols = slice(c * D_MODEL, (c + 1) * D_MODEL)
        w1c = w1_ref[:, cols].astype(BF16)
        hid = [jnp.square(jnp.maximum(mm(a, w1c) + b1_ref[:, cols], 0.0)) for a in x1_bf]
        for h in range(TAIL_SPLIT):
            out = mm(hid[h].astype(BF16), w2_ref[cols, :])
            ff[h] = out if c == 0 else ff[h] + out
    for h, rows in enumerate(parts):
        o_ref[rows, :] = _layer_norm(DEEPNORM_ALPHA * x1[h] + ff[h] + b2_ref[...], g2_ref[...], b2n_ref[...])


def _layer_tail(m, x, weights, glu):
    n = x.shape[0]
    tm = min(n, 512)
    row = pl.BlockSpec((tm, D_MODEL), lambda i: (i, 0))
    ops = [m, x]
    specs = [row, row]
    for wgt in weights:
        if isinstance(wgt, tuple):
            ops.append(wgt[0])
            specs.append(_resident(wgt[0].shape[1:], layer=wgt[1]))
        else:
            a = wgt.reshape(1, -1) if wgt.ndim == 1 else wgt
            ops.append(a)
            specs.append(_resident(a.shape))
    return pl.pallas_call(
        functools.partial(_tail_kernel, glu=glu),
        grid=(n // tm,),
        in_specs=specs,
        out_specs=row,
        out_shape=jax.ShapeDtypeStruct((n, D_MODEL), F32),
        compiler_params=_params("parallel"),
        name="layer_tail_glu" if glu else "layer_tail",
    )(*ops)


def _qkv_kernel(x_ref, w_ref, hsum_ref, q_ref, k_ref, v_ref, norm_ref):
    x = x_ref[...].astype(BF16)
    y = jnp.dot(x, w_ref[:, :2 * D_MODEL], preferred_element_type=F32)
    q = (y[:, :D_MODEL] * (LOG2_E / math.sqrt(SB_HEAD_DIM))).astype(BF16)
    k = y[:, D_MODEL:].astype(BF16)
    q_ref[...] = q
    k_ref[...] = k
    v_ref[...] = jnp.dot(x, w_ref[:, 2 * D_MODEL:], preferred_element_type=F32).astype(BF16)

    def head_norm_max(a):
        sq = jnp.square(a.astype(F32)).astype(BF16)
        return jnp.max(jnp.dot(sq, hsum_ref[...], preferred_element_type=F32), axis=0, keepdims=True)

    norm_ref[0:1, :] = head_norm_max(q)
    norm_ref[1:2, :] = head_norm_max(k)


def _qkv(x, w_qkv):
    n = x.shape[0]
    tm = min(n, QKV_ROWS)
    row = pl.BlockSpec((tm, D_MODEL), lambda i: (i, 0))
    out = jax.ShapeDtypeStruct((n, D_MODEL), BF16)
    hsum = (np.arange(D_MODEL)[:, None] // SB_HEAD_DIM == np.arange(LANES)[None, :]).astype(BF16)
    return pl.pallas_call(
        _qkv_kernel,
        grid=(n // tm,),
        in_specs=[row, _resident(w_qkv.shape), _resident(hsum.shape)],
        out_specs=[row, row, row, pl.BlockSpec((None, 2, LANES), lambda i: (i, 0, 0))],
        out_shape=[out, out, out, jax.ShapeDtypeStruct((n // tm, 2, LANES), F32)],
        compiler_params=_params("parallel"),
        name="qkv_proj",
    )(x, w_qkv, hsum)


def _attn_kernel(q_ref, k_ref, v_ref, qn_ref, kn_ref, ntri_ref, ntrih_ref, o_ref, acc_ref, r_ref, st_ref):
    i = pl.program_id(2)
    tb = ATT_BLOCK
    pairs = q_ref.shape[1] // LANES

    def split_heads(a):
        zero = jnp.zeros_like(a)
        mine = lax.broadcasted_iota(jnp.int32, a.shape, 1) < SB_HEAD_DIM
        return jnp.concatenate([jnp.where(mine, a, zero), jnp.where(mine, zero, a)], axis=0)

    st_ref[1] = jnp.sqrt(jnp.max(qn_ref[0:1, :]) * jnp.max(kn_ref[:, 1:2, :])) * 1.02 + 1e-3

    rr = lax.broadcasted_iota(jnp.int32, (tb, 2 * tb), 0)
    cc = lax.broadcasted_iota(jnp.int32, (tb, 2 * tb), 1)
    causal = jnp.bitwise_and(cc, tb - 1) < rr
    lanes = [slice(p * LANES, (p + 1) * LANES) for p in range(pairs)]

    hr, hk = HEAD_ROWS, HEAD_KEYS
    late = jnp.logical_not(jnp.logical_and(rr < hr, jnp.bitwise_and(cc, tb - 1) >= tb - hk))
    lane_h = lax.broadcasted_iota(jnp.int32, (hr, 2 * hk), 1)
    head_h = [lane_h < hk, lane_h >= hk]

    def cost_of(z, no_overflow):
        if no_overflow:
            return jnp.log2(1.0 + jnp.exp2(z))
        return jnp.maximum(z, 0.0) + jnp.log2(1.0 + 1.0 / jnp.exp2(jnp.abs(z)))

    def hi_lo(c):
        hi = c.astype(BF16)
        return jnp.concatenate([hi, (c - hi.astype(F32)).astype(BF16)], axis=1)

    def visit(blocks, masks, first=False, head_block=None):
        keys = [pl.ds(pl.multiple_of(j * tb, tb), tb) for j in blocks]
        todo = [(p, t) for p in range(pairs) for t in range(len(blocks))]
        z = {pt: lax.dot_general(q_ref[:, lanes[pt[0]]], split_heads(k_ref[keys[pt[1]], lanes[pt[0]]]), NT_DIMS,
                                 preferred_element_type=F32) for pt in todo}
        if head_block is not None:
            hkeys = pl.ds(pl.multiple_of(head_block * tb + (tb - hk), hk), hk)
            zh = [lax.dot_general(q_ref[0:hr, lanes[p]], split_heads(k_ref[hkeys, lanes[p]]), NT_DIMS,
                                  preferred_element_type=F32) for p in range(pairs)]

        def finish(no_overflow):
            cost = {}
            for pt in todo:
                c = cost_of(z[pt], no_overflow)
                cost[pt] = c if masks[pt[1]] is None else jnp.where(masks[pt[1]], c, 0.0)
            suffix = {pt: jnp.dot(hi_lo(cost[pt]), ntri_ref[...], preferred_element_type=F32) for pt in todo}
            if head_block is not None:
                cost_h = [cost_of(zh[p], no_overflow) for p in range(pairs)]
                suffix_h = [jnp.dot(hi_lo(cost_h[p]), ntrih_ref[...], preferred_element_type=F32)
                            for p in range(pairs)]
            r_top = None
            for p in range(pairs):
                r = jnp.zeros((tb, 2 * tb), F32) if first else r_ref[p]
                weights = []
                for t in range(len(blocks)):
                    w = jnp.exp2(z[p, t] + suffix[p, t] + r)
                    if masks[t] is not None:
                        w = jnp.where(masks[t], w, 0.0)
                    weights.append(w.astype(BF16))
                    spent = [jnp.broadcast_to(jnp.sum(cost[p, t][:, h * tb:(h + 1) * tb], axis=1, keepdims=True),
                                              (tb, tb)) for h in range(2)]
                    r = r - jnp.concatenate(spent, axis=1)
                values = jnp.concatenate([split_heads(v_ref[keys[t], lanes[p]]) for t in range(len(blocks))], axis=0)
                out = jnp.dot(jnp.concatenate(weights, axis=1), values, preferred_element_type=F32)
                acc_ref[:, lanes[p]] = out if first else acc_ref[:, lanes[p]] + out
                if head_block is not None:
                    r_h = r[0:hr]
                    w = jnp.exp2(zh[p] + suffix_h[p] + jnp.where(head_h[0], r_h[:, 0:LANES], r_h[:, tb:tb + LANES]))
                    acc_ref[0:hr, lanes[p]] += jnp.dot(w.astype(BF16), split_heads(v_ref[hkeys, lanes[p]]),
                                                       preferred_element_type=F32)
                    spent = [jnp.broadcast_to(jnp.sum(jnp.where(head_h[h], cost_h[p], 0.0), axis=1, keepdims=True),
                                              (hr, tb)) for h in range(2)]
                    r = jnp.concatenate([r_h - jnp.concatenate(spent, axis=1), r[hr:]], axis=0)
                r_ref[p] = r
                r_top = r if r_top is None else jnp.maximum(r_top, r)
            st_ref[0] = jnp.max(r_top)

        if first and len(blocks) > 1:
            bounded = st_ref[1] < BOUNDED_SCORE
            pl.when(bounded)(functools.partial(finish, True))
            pl.when(jnp.logical_not(bounded))(functools.partial(finish, False))
        else:
            finish(False)
        return st_ref[0]

    def unfinished():
        return st_ref[0] + st_ref[1] > EXP2_FLOOR

    has_third = i >= 2

    @pl.when(has_third)
    def _():
        visit([i, i - 1], [causal, None], first=True, head_block=i - 2)

    @pl.when(jnp.logical_and(has_third, unfinished()))
    def _():
        visit([i - 2], [late])

    @pl.when(jnp.logical_not(has_third))
    def _():
        visit([i], [causal], first=True)

    def more(c):
        j, r_max = c
        return jnp.logical_and(j >= 0, r_max + st_ref[1] > EXP2_FLOOR)

    def step(c):
        j, _ = c
        return j - 1, visit([j], [None])

    lax.while_loop(more, step, (jnp.where(has_third, i - 3, i - 1), st_ref[0]))
    o_ref[...] = acc_ref[...].astype(o_ref.dtype)


def _neg_suffix_matrix(keys):
    idx = np.arange(2 * keys)
    later_same_head = (idx[:, None] // keys == idx[None, :] // keys) & (idx[:, None] >= idx[None, :])
    m = np.where(later_same_head, -1.0, 0.0).astype(np.float32)
    return np.concatenate([m, m], axis=0).astype(BF16)


def _attention(q, k, v, norms, batch):
    n = q.shape[0]
    seq = n // batch
    tile = n // norms.shape[0]
    tb = ATT_BLOCK
    nq = seq // tb
    width = ATT_PAIRS * LANES
    ntri = _neg_suffix_matrix(tb)
    ntrih = _neg_suffix_matrix(HEAD_KEYS)
    qn_spec = pl.BlockSpec((None, 2, LANES), lambda b, h, i: ((b * nq + i) * tb // tile, 0, 0))
    kn_spec = pl.BlockSpec((seq // tile, 2, LANES), lambda b, h, i: (b, 0, 0))
    kv_spec = pl.BlockSpec((seq, width), lambda b, h, i: (b, h))
    q_spec = pl.BlockSpec((tb, width), lambda b, h, i: (b * nq + i, h))
    return pl.pallas_call(
        _attn_kernel,
        grid=(batch, D_MODEL // width, nq),
        in_specs=[q_spec, kv_spec, kv_spec, qn_spec, kn_spec, _resident(ntri.shape), _resident(ntrih.shape)],
        out_specs=q_spec,
        out_shape=jax.ShapeDtypeStruct((n, D_MODEL), BF16),
        scratch_shapes=[pltpu.VMEM((tb, width), F32), pltpu.VMEM((ATT_PAIRS, tb, 2 * tb), F32),
                        pltpu.SMEM((2,), F32)],
        compiler_params=_params("parallel", "parallel", "arbitrary"),
        name="sb_attention",
    )(q, k, v, norms, norms, ntri, ntrih)


def kernel(x, s5_w_in, s5_lambda_re, s5_lambda_im, s5_b_re, s5_b_im, s5_c_re, s5_c_im, s5_d, s5_log_step,
           s5_w_glu, s5_b_glu, s5_w_out, sb_w_kv, sb_w_q, sb_w_out, mlp_w1, mlp_b1, mlp_w2, mlp_b2,
           ln_mix_g, ln_mix_b, ln_mlp_g, ln_mlp_b):
    batch, seq, d = x.shape
    n = batch * seq
    bf = lambda a: a.astype(BF16)
    x0 = x.reshape(n, d)
    w1, w2 = mlp_w1, bf(mlp_w2)

    car, cai, wr, wi, pwr, pwi = _s5_prep(s5_lambda_re[0], s5_lambda_im[0], s5_log_step[0],
                                          s5_b_re[0], s5_b_im[0], s5_c_re[0], s5_c_im[0])
    d_tiled = jnp.tile(s5_d[0].reshape(N_TILES, 1, LANES), (1, 1, KB))
    u = _u_proj(x0, bf(s5_w_in[0]))
    y = _ssm(u, car, cai, wr, wi, pwr, pwi, d_tiled, batch)
    x1 = _layer_tail(
        y, x0,
        [bf(s5_w_glu[0]), s5_b_glu[0], bf(s5_w_out[0]), ln_mix_g[0], ln_mix_b[0],
         (w1, 0), mlp_b1[0], (w2, 0), mlp_b2[0], ln_mlp_g[0], ln_mlp_b[0]],
        glu=True)

    q, k, v, norms = _qkv(x1, bf(jnp.concatenate([sb_w_q[0], sb_w_kv], axis=1)))
    o = _attention(q, k, v, norms, batch)
    out = _layer_tail(
        o, x1,
        [bf(sb_w_out[0]), ln_mix_g[1], ln_mix_b[1],
         (w1, 1), mlp_b1[1], (w2, 1), mlp_b2[1], ln_mlp_g[1], ln_mlp_b[1]],
        glu=False)
    return out.reshape(batch, seq, d)
```

```python
import functools
import math

import jax
import jax.numpy as jnp
import numpy as np
from jax import lax
from jax.experimental import pallas as pl
from jax.experimental.pallas import tpu as pltpu

F32 = jnp.float32
BF16 = jnp.bfloat16

D_MODEL = 1024
DEPTH = 2
S5_GROUP = 16
S5_GROUPS = D_MODEL // S5_GROUP
S5_STATE = 64
SB_HEADS = 16
SB_HEAD_DIM = D_MODEL // SB_HEADS
D_FF = 4 * D_MODEL
DEEPNORM_ALPHA = (2.0 * DEPTH) ** 0.25
LN_EPS = 1e-5

LANES = 128
SUBLANES = 8
KB = 16
GROUPS_PER_TILE = LANES // S5_GROUP
N_TILES = D_MODEL // LANES
TILE_STATE = GROUPS_PER_TILE * S5_STATE
ALL_STATE = S5_GROUPS * S5_STATE
VMEM_LIMIT = 56 * 1024 * 1024
TAIL_SPLIT = 2
ATT_BLOCK = 128
ATT_PAIRS = 8
QKV_ROWS = 1024
HEAD_ROWS = 64
HEAD_KEYS = 64
EXP2_FLOOR = -130.0
BOUNDED_SCORE = 120.0
LOG2_E = 1.0 / math.log(2.0)
NT_DIMS = (((1,), (1,)), ((), ()))


def _params(*sem):
    return pltpu.CompilerParams(dimension_semantics=sem, vmem_limit_bytes=VMEM_LIMIT)


def _resident(shape, layer=None):
    zeros = (0,) * len(shape)
    if layer is None:
        return pl.BlockSpec(shape, lambda *_: zeros, pipeline_mode=pl.Buffered(1))
    return pl.BlockSpec((None,) + shape, lambda *_: (layer,) + zeros, pipeline_mode=pl.Buffered(1))


def _cmul(ar, ai, br, bi):
    return ar * br - ai * bi, ar * bi + ai * br


def _s5_prep_kernel(lr_ref, li_ref, ls_ref, br_ref, bi_ref, cr_ref, ci_ref,
                    car_ref, cai_ref, wr_ref, wi_ref, pwr_ref, pwi_ref,
                    a1r_ref, a1i_ref, pr_ref, pi_ref, bbr_ref, bbi_ref):
    tau = pl.program_id(0)

    @pl.when(tau == 0)
    def _():
        lr = lr_ref[...]
        li = li_ref[...]
        dt = jnp.exp(ls_ref[...])
        mag = jnp.exp(lr * dt)
        ang = li * dt
        a_re = mag * jnp.cos(ang)
        a_im = mag * jnp.sin(ang)
        nr = a_re - 1.0
        ni = a_im
        den = lr * lr + li * li
        f_re = (nr * lr + ni * li) / den
        f_im = (ni * lr - nr * li) / den
        bbr, bbi = _cmul(f_re, f_im, br_ref[...], bi_ref[...])
        bbr_ref[...] = bbr
        bbi_ref[...] = bbi
        a1r_ref[...] = a_re
        a1i_ref[...] = a_im
        pr_ref[...] = jnp.ones_like(a_re)
        pi_ref[...] = jnp.zeros_like(a_im)

    p_re = pr_ref[...]
    p_im = pi_ref[...]
    car, cai = _cmul(cr_ref[...], ci_ref[...], p_re, p_im)
    car_ref[...] = car
    cai_ref[...] = -cai
    wr, wi = _cmul(p_re, p_im, bbr_ref[...], bbi_ref[...])
    wr_ref[...] = wr
    wi_ref[...] = wi

    @pl.when(tau == KB)
    def _():
        q_re, q_im = p_re, p_im
        for i in range(SUBLANES):
            pwr_ref[i:i + 1, :] = q_re
            pwi_ref[i:i + 1, :] = q_im
            q_re, q_im = _cmul(q_re, q_im, p_re, p_im)

    n_re, n_im = _cmul(p_re, p_im, a1r_ref[...], a1i_ref[...])
    pr_ref[...] = n_re
    pi_ref[...] = n_im


def _s5_prep(lam_re, lam_im, log_step, b_re, b_im, c_re, c_im):
    g, p, h = S5_GROUPS, S5_STATE, S5_GROUP
    lr = lam_re.reshape(1, ALL_STATE)
    li = lam_im.reshape(1, ALL_STATE)
    ls = jnp.broadcast_to(log_step[:, None], (g, p)).reshape(1, ALL_STATE)
    br = b_re.transpose(2, 0, 1).reshape(h, ALL_STATE)
    bi = b_im.transpose(2, 0, 1).reshape(h, ALL_STATE)
    cr = c_re.transpose(1, 0, 2).reshape(h, ALL_STATE)
    ci = c_im.transpose(1, 0, 2).reshape(h, ALL_STATE)
    small = pl.BlockSpec((1, ALL_STATE), lambda t: (0, 0))
    big = pl.BlockSpec((h, ALL_STATE), lambda t: (0, 0))
    step = pl.BlockSpec((None, h, ALL_STATE), lambda t: (t, 0, 0))
    pw = pl.BlockSpec((SUBLANES, ALL_STATE), lambda t: (0, 0))
    out3 = jax.ShapeDtypeStruct((KB + 1, h, ALL_STATE), F32)
    outp = jax.ShapeDtypeStruct((SUBLANES, ALL_STATE), F32)
    return pl.pallas_call(
        _s5_prep_kernel,
        grid=(KB + 1,),
        in_specs=[small, small, small, big, big, big, big],
        out_specs=[step, step, step, step, pw, pw],
        out_shape=[out3, out3, out3, out3, outp, outp],
        scratch_shapes=[pltpu.VMEM((1, ALL_STATE), F32)] * 4 + [pltpu.VMEM((h, ALL_STATE), F32)] * 2,
        compiler_params=_params("arbitrary"),
        name="s5_prep",
    )(lr, li, ls, br, bi, cr, ci)


def _uproj_kernel(x_ref, w_ref, o_ref):
    o_ref[...] = jnp.dot(x_ref[...].astype(BF16), w_ref[...], preferred_element_type=F32)


def _u_proj(x, w_in):
    n = x.shape[0]
    tm = min(n, 2048)
    row = pl.BlockSpec((tm, D_MODEL), lambda i: (i, 0))
    return pl.pallas_call(
        _uproj_kernel,
        grid=(n // tm,),
        in_specs=[row, _resident((D_MODEL, D_MODEL))],
        out_specs=row,
        out_shape=jax.ShapeDtypeStruct((n, D_MODEL), F32),
        compiler_params=_params("parallel"),
        name="s5_u_proj",
    )(x, w_in)


def _shift_rows(x, k, row):
    return jnp.where(row >= k, pltpu.roll(x, k, 0), 0.0)


def _ssm_build_operators(car_ref, cai_ref, wr_ref, wi_ref, t_ref, win_ref, voutt_ref):
    ts = TILE_STATE
    row_group = lax.broadcasted_iota(jnp.int32, (LANES, ts), 0) // S5_GROUP
    lane_group = lax.broadcasted_iota(jnp.int32, (LANES, ts), 1) // S5_STATE
    same_group = row_group == lane_group

    def expand(re16, im16):
        tile = lambda a: jnp.where(same_group, jnp.concatenate([a] * GROUPS_PER_TILE, axis=0), 0.0)
        return jnp.concatenate([tile(re16), tile(im16)], axis=1)

    t_ref[...] = jnp.zeros_like(t_ref)
    bbar = expand(wr_ref[0], wi_ref[0]).astype(BF16)
    for tau in range(KB + 1):
        ca = expand(car_ref[tau], cai_ref[tau]).astype(BF16)
        if tau >= 1:
            voutt_ref[(tau - 1) * LANES:tau * LANES, :] = ca
        if tau < KB:
            lag = lax.dot_general(bbar, ca, NT_DIMS, preferred_element_type=F32).astype(BF16)
            for j in range(KB - tau):
                t_ref[j * LANES:(j + 1) * LANES, (j + tau) * LANES:(j + tau + 1) * LANES] = lag
            win_ref[(KB - 1 - tau) * LANES:(KB - tau) * LANES, :] = expand(wr_ref[tau], wi_ref[tau]).astype(BF16)


def _ssm_kernel(u_ref, car_ref, cai_ref, wr_ref, wi_ref, pwr_ref, pwi_ref, d_ref, y_ref,
                s_ref, yin_ref, t_ref, win_ref, voutt_ref):
    nb = u_ref.shape[0] // KB
    ts = TILE_STATE

    @pl.when(pl.program_id(1) == 0)
    def _():
        _ssm_build_operators(car_ref, cai_ref, wr_ref, wi_ref, t_ref, win_ref, voutt_ref)

    u_cat = jnp.concatenate([u_ref[pl.ds(j, nb, stride=KB), :] for j in range(KB)], axis=1)
    u_bf = u_cat.astype(BF16)
    width = 2 * LANES
    chunks = [slice(c * width, (c + 1) * width) for c in range(KB * LANES // width)]

    def y_from_inputs(c):
        cols = chunks[c]
        yin_ref[:, cols] = jnp.dot(u_bf[:, :cols.stop], t_ref[0:cols.stop, cols], preferred_element_type=F32)

    early = len(chunks) // 2
    for c in range(early):
        y_from_inputs(c)
    s_ref[...] = jnp.dot(u_bf, win_ref[...], preferred_element_type=F32)
    row = lax.broadcasted_iota(jnp.int32, (SUBLANES, ts), 0)
    pw_re = pwr_ref[...]
    pw_im = pwi_ref[...]

    def group(r, carry):
        c_re, c_im = carry
        rows = pl.ds(r * SUBLANES, SUBLANES)
        x_re = s_ref[rows, 0:ts]
        x_im = s_ref[rows, ts:2 * ts]
        for k in (1, 2, 4):
            a_re = pw_re[k - 1:k]
            a_im = pw_im[k - 1:k]
            sh_re = _shift_rows(x_re, k, row)
            sh_im = _shift_rows(x_im, k, row)
            x_re, x_im = x_re + a_re * sh_re - a_im * sh_im, x_im + a_re * sh_im + a_im * sh_re
        x_re, x_im = x_re + pw_re * c_re - pw_im * c_im, x_im + pw_re * c_im + pw_im * c_re
        s_ref[rows, 0:ts] = jnp.where(row >= 1, pltpu.roll(x_re, 1, 0), c_re)
        s_ref[rows, ts:2 * ts] = jnp.where(row >= 1, pltpu.roll(x_im, 1, 0), c_im)
        return x_re[SUBLANES - 1:SUBLANES], x_im[SUBLANES - 1:SUBLANES]

    groups = nb // SUBLANES
    carry = (jnp.zeros((1, ts), F32),) * 2
    for r in range(groups):
        for c in range(early, len(chunks)):
            if (c - early) * groups // (len(chunks) - early) == r:
                y_from_inputs(c)
        carry = group(r, carry)
    s_bf = s_ref[...].astype(BF16)
    for c, cols in enumerate(chunks):
        y = yin_ref[:, cols] + lax.dot_general(s_bf, voutt_ref[cols, :], NT_DIMS, preferred_element_type=F32)
        y = y + d_ref[:, cols] * u_cat[:, cols]
        for jj in range(width // LANES):
            y_ref[pl.ds(c * (width // LANES) + jj, nb, stride=KB), :] = y[:, jj * LANES:(jj + 1) * LANES]


def _ssm(u, car, cai, wr, wi, pwr, pwi, d_tiled, batch):
    seq = u.shape[0] // batch
    nb = seq // KB
    factor = pl.BlockSpec((KB + 1, S5_GROUP, TILE_STATE), lambda c, b: (0, 0, c))
    power = pl.BlockSpec((SUBLANES, TILE_STATE), lambda c, b: (0, c))
    io = pl.BlockSpec((seq, LANES), lambda c, b: (b, c))
    return pl.pallas_call(
        _ssm_kernel,
        grid=(N_TILES, batch),
        in_specs=[io, factor, factor, factor, factor, power, power,
                  pl.BlockSpec((None, 1, KB * LANES), lambda c, b: (c, 0, 0))],
        out_specs=io,
        out_shape=jax.ShapeDtypeStruct(u.shape, F32),
        scratch_shapes=[pltpu.VMEM((nb, 2 * TILE_STATE), F32),
                        pltpu.VMEM((nb, KB * LANES), F32),
                        pltpu.VMEM((KB * LANES, KB * LANES), BF16),
                        pltpu.VMEM((KB * LANES, 2 * TILE_STATE), BF16),
                        pltpu.VMEM((KB * LANES, 2 * TILE_STATE), BF16)],
        compiler_params=_params("parallel", "arbitrary"),
        name="s5_ssm",
    )(u, car, cai, wr, wi, pwr, pwi, d_tiled)


def _layer_norm(z, g, b):
    mu = jnp.mean(z, axis=-1, keepdims=True)
    zc = z - mu
    var = jnp.mean(zc * zc, axis=-1, keepdims=True)
    return zc * lax.rsqrt(var + LN_EPS) * g + b


def _tail_kernel(*refs, glu):
    if glu:
        (m_ref, x_ref, wglu_ref, bglu_ref, wout_ref, g1_ref, b1n_ref,
         w1_ref, b1_ref, w2_ref, b2_ref, g2_ref, b2n_ref, o_ref) = refs
    else:
        (m_ref, x_ref, wout_ref, g1_ref, b1n_ref,
         w1_ref, b1_ref, w2_ref, b2_ref, g2_ref, b2n_ref, o_ref) = refs
    tm = x_ref.shape[0]
    parts = [slice(h * tm // TAIL_SPLIT, (h + 1) * tm // TAIL_SPLIT) for h in range(TAIL_SPLIT)]
    mm = lambda a, b: jnp.dot(a, b, preferred_element_type=F32)
    if glu:
        g = [jax.nn.gelu(m_ref[rows, :]) for rows in parts]
        gate = [mm(gh.astype(BF16), wglu_ref[...]) + bglu_ref[...] for gh in g]
        m = [(gh * jax.nn.sigmoid(th)).astype(BF16) for gh, th in zip(g, gate)]
    else:
        m = [m_ref[rows, :] for rows in parts]
    mix = [mm(mh, wout_ref[...]) for mh in m]
    x1 = [_layer_norm(DEEPNORM_ALPHA * x_ref[rows, :] + mh, g1_ref[...], b1n_ref[...])
          for rows, mh in zip(parts, mix)]
    x1_bf = [a.astype(BF16) for a in x1]
    ff = [None] * TAIL_SPLIT
    for c in range(D_FF // D_MODEL):
        cols = slice(c * D_MODEL, (c + 1) * D_MODEL)
        w1c = w1_ref[:, cols].astype(BF16)
        hid = [jnp.square(jnp.maximum(mm(a, w1c) + b1_ref[:, cols], 0.0)) for a in x1_bf]
        for h in range(TAIL_SPLIT):
            out = mm(hid[h].astype(BF16), w2_ref[cols, :])
            ff[h] = out if c == 0 else ff[h] + out
    for h, rows in enumerate(parts):
        o_ref[rows, :] = _layer_norm(DEEPNORM_ALPHA * x1[h] + ff[h] + b2_ref[...], g2_ref[...], b2n_ref[...])


def _layer_tail(m, x, weights, glu):
    n = x.shape[0]
    tm = min(n, 512)
    row = pl.BlockSpec((tm, D_MODEL), lambda i: (i, 0))
    ops = [m, x]
    specs = [row, row]
    for wgt in weights:
        if isinstance(wgt, tuple):
            ops.append(wgt[0])
            specs.append(_resident(wgt[0].shape[1:], layer=wgt[1]))
        else:
            a = wgt.reshape(1, -1) if wgt.ndim == 1 else wgt
            ops.append(a)
            specs.append(_resident(a.shape))
    return pl.pallas_call(
        functools.partial(_tail_kernel, glu=glu),
        grid=(n // tm,),
        in_specs=specs,
        out_specs=row,
        out_shape=jax.ShapeDtypeStruct((n, D_MODEL), F32),
        compiler_params=_params("parallel"),
        name="layer_tail_glu" if glu else "layer_tail",
    )(*ops)


def _qkv_kernel(x_ref, w_ref, hsum_ref, q_ref, k_ref, v_ref, norm_ref):
    x = x_ref[...].astype(BF16)
    y = jnp.dot(x, w_ref[:, :2 * D_MODEL], preferred_element_type=F32)
    q = (y[:, :D_MODEL] * (LOG2_E / math.sqrt(SB_HEAD_DIM))).astype(BF16)
    k = y[:, D_MODEL:].astype(BF16)
    q_ref[...] = q
    k_ref[...] = k
    v_ref[...] = jnp.dot(x, w_ref[:, 2 * D_MODEL:], preferred_element_type=F32).astype(BF16)

    def head_norm_max(a):
        sq = jnp.square(a.astype(F32)).astype(BF16)
        return jnp.max(jnp.dot(sq, hsum_ref[...], preferred_element_type=F32), axis=0, keepdims=True)

    norm_ref[0:1, :] = head_norm_max(q)
    norm_ref[1:2, :] = head_norm_max(k)


def _qkv(x, w_qkv):
    n = x.shape[0]
    tm = min(n, QKV_ROWS)
    row = pl.BlockSpec((tm, D_MODEL), lambda i: (i, 0))
    out = jax.ShapeDtypeStruct((n, D_MODEL), BF16)
    hsum = (np.arange(D_MODEL)[:, None] // SB_HEAD_DIM == np.arange(LANES)[None, :]).astype(BF16)
    return pl.pallas_call(
        _qkv_kernel,
        grid=(n // tm,),
        in_specs=[row, _resident(w_qkv.shape), _resident(hsum.shape)],
        out_specs=[row, row, row, pl.BlockSpec((None, 2, LANES), lambda i: (i, 0, 0))],
        out_shape=[out, out, out, jax.ShapeDtypeStruct((n // tm, 2, LANES), F32)],
        compiler_params=_params("parallel"),
        name="qkv_proj",
    )(x, w_qkv, hsum)


def _attn_kernel(q_ref, k_ref, v_ref, qn_ref, kn_ref, ntri_ref, ntrih_ref, o_ref, acc_ref, r_ref, st_ref):
    i = pl.program_id(2)
    tb = ATT_BLOCK
    pairs = q_ref.shape[1] // LANES

    def split_heads(a):
        zero = jnp.zeros_like(a)
        mine = lax.broadcasted_iota(jnp.int32, a.shape, 1) < SB_HEAD_DIM
        return jnp.concatenate([jnp.where(mine, a, zero), jnp.where(mine, zero, a)], axis=0)

    st_ref[1] = jnp.sqrt(jnp.max(qn_ref[0:1, :]) * jnp.max(kn_ref[:, 1:2, :])) * 1.02 + 1e-3

    rr = lax.broadcasted_iota(jnp.int32, (tb, 2 * tb), 0)
    cc = lax.broadcasted_iota(jnp.int32, (tb, 2 * tb), 1)
    causal = jnp.bitwise_and(cc, tb - 1) < rr
    lanes = [slice(p * LANES, (p + 1) * LANES) for p in range(pairs)]

    hr, hk = HEAD_ROWS, HEAD_KEYS
    late = jnp.logical_not(jnp.logical_and(rr < hr, jnp.bitwise_and(cc, tb - 1) >= tb - hk))
    lane_h = lax.broadcasted_iota(jnp.int32, (hr, 2 * hk), 1)
    head_h = [lane_h < hk, lane_h >= hk]

    def cost_of(z, no_overflow):
        if no_overflow:
            return jnp.log2(1.0 + jnp.exp2(z))
        return jnp.maximum(z, 0.0) + jnp.log2(1.0 + 1.0 / jnp.exp2(jnp.abs(z)))

    def hi_lo(c):
        hi = c.astype(BF16)
        return jnp.concatenate([hi, (c - hi.astype(F32)).astype(BF16)], axis=1)

    def visit(blocks, masks, first=False, head_block=None):
        keys = [pl.ds(pl.multiple_of(j * tb, tb), tb) for j in blocks]
        todo = [(p, t) for p in range(pairs) for t in range(len(blocks))]
        z = {pt: lax.dot_general(q_ref[:, lanes[pt[0]]], split_heads(k_ref[keys[pt[1]], lanes[pt[0]]]), NT_DIMS,
                                 preferred_element_type=F32) for pt in todo}
        if head_block is not None:
            hkeys = pl.ds(pl.multiple_of(head_block * tb + (tb - hk), hk), hk)
            zh = [lax.dot_general(q_ref[0:hr, lanes[p]], split_heads(k_ref[hkeys, lanes[p]]), NT_DIMS,
                                  preferred_element_type=F32) for p in range(pairs)]

        def finish(no_overflow):
            cost = {}
            for pt in todo:
                c = cost_of(z[pt], no_overflow)
                cost[pt] = c if masks[pt[1]] is None else jnp.where(masks[pt[1]], c, 0.0)
            suffix = {pt: jnp.dot(hi_lo(cost[pt]), ntri_ref[...], preferred_element_type=F32) for pt in todo}
            if head_block is not None:
                cost_h = [cost_of(zh[p], no_overflow) for p in range(pairs)]
                suffix_h = [jnp.dot(hi_lo(cost_h[p]), ntrih_ref[...], preferred_element_type=F32)
                            for p in range(pairs)]
            r_top = None
            for p in range(pairs):
                r = jnp.zeros((tb, 2 * tb), F32) if first else r_ref[p]
                weights = []
                for t in range(len(blocks)):
                    w = jnp.exp2(z[p, t] + suffix[p, t] + r)
                    if masks[t] is not None:
                        w = jnp.where(masks[t], w, 0.0)
                    weights.append(w.astype(BF16))
                    spent = [jnp.broadcast_to(jnp.sum(cost[p, t][:, h * tb:(h + 1) * tb], axis=1, keepdims=True),
                                              (tb, tb)) for h in range(2)]
                    r = r - jnp.concatenate(spent, axis=1)
                values = jnp.concatenate([split_heads(v_ref[keys[t], lanes[p]]) for t in range(len(blocks))], axis=0)
                out = jnp.dot(jnp.concatenate(weights, axis=1), values, preferred_element_type=F32)
                acc_ref[:, lanes[p]] = out if first else acc_ref[:, lanes[p]] + out
                if head_block is not None:
                    r_h = r[0:hr]
                    w = jnp.exp2(zh[p] + suffix_h[p] + jnp.where(head_h[0], r_h[:, 0:LANES], r_h[:, tb:tb + LANES]))
                    acc_ref[0:hr, lanes[p]] += jnp.dot(w.astype(BF16), split_heads(v_ref[hkeys, lanes[p]]),
                                                       preferred_element_type=F32)
                    spent = [jnp.broadcast_to(jnp.sum(jnp.where(head_h[h], cost_h[p], 0.0), axis=1, keepdims=True),
                                              (hr, tb)) for h in range(2)]
                    r = jnp.concatenate([r_h - jnp.concatenate(spent, axis=1), r[hr:]], axis=0)
                r_ref[p] = r
                r_top = r if r_top is None else jnp.maximum(r_top, r)
            st_ref[0] = jnp.max(r_top)

        if first and len(blocks) > 1:
            bounded = st_ref[1] < BOUNDED_SCORE
            pl.when(bounded)(functools.partial(finish, True))
            pl.when(jnp.logical_not(bounded))(functools.partial(finish, False))
        else:
            finish(False)
        return st_ref[0]

    def unfinished():
        return st_ref[0] + st_ref[1] > EXP2_FLOOR

    has_third = i >= 2

    @pl.when(has_third)
    def _():
        visit([i, i - 1], [causal, None], first=True, head_block=i - 2)

    @pl.when(jnp.logical_and(has_third, unfinished()))
    def _():
        visit([i - 2], [late])

    @pl.when(jnp.logical_not(has_third))
    def _():
        visit([i], [causal], first=True)

    def more(c):
        j, r_max = c
        return jnp.logical_and(j >= 0, r_max + st_ref[1] > EXP2_FLOOR)

    def step(c):
        j, _ = c
        return j - 1, visit([j], [None])

    lax.while_loop(more, step, (jnp.where(has_third, i - 3, i - 1), st_ref[0]))
    o_ref[...] = acc_ref[...].astype(o_ref.dtype)


def _neg_suffix_matrix(keys):
    idx = np.arange(2 * keys)
    later_same_head = (idx[:, None] // keys == idx[None, :] // keys) & (idx[:, None] >= idx[None, :])
    m = np.where(later_same_head, -1.0, 0.0).astype(np.float32)
    return np.concatenate([m, m], axis=0).astype(BF16)


def _attention(q, k, v, norms, batch):
    n = q.shape[0]
    seq = n // batch
    tile = n // norms.shape[0]
    tb = ATT_BLOCK
    nq = seq // tb
    width = ATT_PAIRS * LANES
    ntri = _neg_suffix_matrix(tb)
    ntrih = _neg_suffix_matrix(HEAD_KEYS)
    qn_spec = pl.BlockSpec((None, 2, LANES), lambda b, h, i: ((b * nq + i) * tb // tile, 0, 0))
    kn_spec = pl.BlockSpec((seq // tile, 2, LANES), lambda b, h, i: (b, 0, 0))
    kv_spec = pl.BlockSpec((seq, width), lambda b, h, i: (b, h))
    q_spec = pl.BlockSpec((tb, width), lambda b, h, i: (b * nq + i, h))
    return pl.pallas_call(
        _attn_kernel,
        grid=(batch, D_MODEL // width, nq),
        in_specs=[q_spec, kv_spec, kv_spec, qn_spec, kn_spec, _resident(ntri.shape), _resident(ntrih.shape)],
        out_specs=q_spec,
        out_shape=jax.ShapeDtypeStruct((n, D_MODEL), BF16),
        scratch_shapes=[pltpu.VMEM((tb, width), F32), pltpu.VMEM((ATT_PAIRS, tb, 2 * tb), F32),
                        pltpu.SMEM((2,), F32)],
        compiler_params=_params("parallel", "parallel", "arbitrary"),
        name="sb_attention",
    )(q, k, v, norms, norms, ntri, ntrih)


def kernel(x, s5_w_in, s5_lambda_re, s5_lambda_im, s5_b_re, s5_b_im, s5_c_re, s5_c_im, s5_d, s5_log_step,
           s5_w_glu, s5_b_glu, s5_w_out, sb_w_kv, sb_w_q, sb_w_out, mlp_w1, mlp_b1, mlp_w2, mlp_b2,
           ln_mix_g, ln_mix_b, ln_mlp_g, ln_mlp_b):
    batch, seq, d = x.shape
    n = batch * seq
    bf = lambda a: a.astype(BF16)
    x0 = x.reshape(n, d)
    w1, w2 = mlp_w1, bf(mlp_w2)

    car, cai, wr, wi, pwr, pwi = _s5_prep(s5_lambda_re[0], s5_lambda_im[0], s5_log_step[0],
                                          s5_b_re[0], s5_b_im[0], s5_c_re[0], s5_c_im[0])
    d_tiled = jnp.tile(s5_d[0].reshape(N_TILES, 1, LANES), (1, 1, KB))
    u = _u_proj(x0, bf(s5_w_in[0]))
    y = _ssm(u, car, cai, wr, wi, pwr, pwi, d_tiled, batch)
    x1 = _layer_tail(
        y, x0,
        [bf(s5_w_glu[0]), s5_b_glu[0], bf(s5_w_out[0]), ln_mix_g[0], ln_mix_b[0],
         (w1, 0), mlp_b1[0], (w2, 0), mlp_b2[0], ln_mlp_g[0], ln_mlp_b[0]],
        glu=True)

    q, k, v, norms = _qkv(x1, bf(jnp.concatenate([sb_w_q[0], sb_w_kv], axis=1)))
    o = _attention(q, k, v, norms, batch)
    out = _layer_tail(
        o, x1,
        [bf(sb_w_out[0]), ln_mix_g[1], ln_mix_b[1],
         (w1, 1), mlp_b1[1], (w2, 1), mlp_b2[1], ln_mlp_g[1], ln_mlp_b[1]],
        glu=False)
    return out.reshape(batch, seq, d)
```
